```python
import functools
import jax, jax.numpy as jnp
from jax import lax
import numpy as np

D_MODEL = 2048
BATCH = 4
SEQ = 8192
DEPTH = 1
DEC_BATCH = 16
DEC_SEQ = 64
PAST_LEN = 4096

CHUNK = 64
N_META = 16
MIX_WIDTH = D_MODEL
HEAD_DIM = 64
W_A = MIX_WIDTH // 2
W_B = MIX_WIDTH - W_A
H_A = W_A // HEAD_DIM
H_B = W_B // HEAD_DIM
R_DECAY = 64
R_A = 64
R_GATE = 160
W_SHIFT = 3 * W_A + R_DECAY + R_A + R_GATE
P_COLS = W_SHIFT + 3 * W_B + 2 * D_MODEL
SHIFT_SPLITS = (W_A, 2 * W_A, 3 * W_A, 3 * W_A + R_DECAY, 3 * W_A + R_DECAY + R_A)
PROJ_SPLITS = (W_SHIFT, W_SHIFT + W_B, W_SHIFT + 2 * W_B, W_SHIFT + 3 * W_B)
SB_BLOCK = 128
SB_SCALE = HEAD_DIM ** -0.5
N_GROUPS = 4
EXPERTS_PER_GROUP = 8
N_EXPERTS = N_GROUPS * EXPERTS_PER_GROUP
TOP_K = 2
D_EXPERT = D_MODEL // 4
ALPHA = (2 * DEPTH) ** 0.25
BETA = (8 * DEPTH) ** -0.25
LN_EPS = 1e-5
GN_EPS = 64e-5

kernel_name = 'rwkv7_stickbreak_hmoe_stream_step'


def layer_norm(x, g, b):
    xf = x.astype(jnp.float32)
    mu = jnp.mean(xf, axis=-1, keepdims=True)
    var = jnp.mean(jnp.square(xf - mu), axis=-1, keepdims=True)
    return (xf - mu) * lax.rsqrt(var + LN_EPS) * g + b


def rwkv_inputs(p, prev, mu, w0, w_up, a0, a_up, g_up, k_k, k_a):
    bsz, t = p.shape[:2]
    p = p.astype(jnp.float32)
    shifted = jnp.concatenate([prev.astype(jnp.float32), p[:, :-1]], axis=1)
    xm = p + (shifted - p) * mu
    r, k, v, xw, xa, xg = jnp.split(xm, SHIFT_SPLITS, axis=-1)
    w = -jax.nn.softplus(-(w0 + jnp.tanh(xw) @ w_up)) - 0.5
    decay = jnp.exp(-jnp.exp(w))
    a = jax.nn.sigmoid(a0 + xa @ a_up)
    g = jax.nn.sigmoid(xg) @ g_up
    heads = lambda z: z.reshape(bsz, t, H_A, HEAD_DIM).astype(jnp.float32)
    kk = heads(k * k_k)
    kk = kk / jnp.maximum(jnp.linalg.norm(kk, axis=-1, keepdims=True), 1e-12)
    k = k * (1.0 + (a - 1.0) * k_a)
    return heads(r), heads(decay), heads(k), heads(v), kk, heads(a), g


def rwkv_step(state, inp):
    r, decay, k, v, kk, a = inp
    sa = jnp.einsum('bhvk,bhk->bhv', state, -kk)
    state = (state * decay[:, :, None, :] + sa[..., None] * (kk * a)[:, :, None, :]
             + v[..., None] * k[:, :, None, :])
    return state, jnp.einsum('bhvk,bhk->bhv', state, r)


def rwkv_scan(state0, r, decay, k, v, kk, a):
    xs = tuple(jnp.moveaxis(z, 1, 0) for z in (r, decay, k, v, kk, a))
    state, y = lax.scan(rwkv_step, state0.astype(jnp.float32), xs)
    return state, jnp.moveaxis(y, 0, 1)


def rwkv_output(y, r, k, v, g, r_k, lnx_g, lnx_b):
    bsz, t = y.shape[:2]
    mu = jnp.mean(y, axis=-1, keepdims=True)
    var = jnp.mean(jnp.square(y - mu), axis=-1, keepdims=True)
    yn = ((y - mu) * lax.rsqrt(var + GN_EPS)).reshape(bsz, t, W_A) * lnx_g + lnx_b
    bonus = (jnp.sum(r * k * r_k, axis=-1, keepdims=True) * v).reshape(bsz, t, W_A)
    return (yn + bonus) * g


def stick_breaking(q, k, v, q_pos, k_pos):
    z = jnp.einsum('bqhe,bkhe->bhqk', q, k).astype(jnp.float32) * SB_SCALE
    visible = k_pos[None, :] < q_pos[:, None]
    log_beta = jax.nn.log_sigmoid(z)
    log_rest = jnp.where(visible, log_beta - z, 0.0)
    log_between = lax.cumsum(log_rest, axis=3, reverse=True) - log_rest
    weight = jnp.where(visible, jnp.exp(log_beta + log_between), 0.0)
    return jnp.einsum('bhqk,bkhe->bqhe', weight, v.astype(jnp.float32))


def sb_prompt(q, k, v):
    bsz, t = q.shape[:2]
    n_blk = -(-t // SB_BLOCK)
    padw = ((0, 0), (0, n_blk * SB_BLOCK - t), (0, 0), (0, 0))
    qp, kp, vp = jnp.pad(q, padw), jnp.pad(k, padw), jnp.pad(v, padw)
    k_pos = jnp.arange(n_blk * SB_BLOCK)

    def block(i):
        start = i * SB_BLOCK
        qb = lax.dynamic_slice_in_dim(qp, start, SB_BLOCK, axis=1)
        return stick_breaking(qb, kp, vp, start + jnp.arange(SB_BLOCK), k_pos)

    o = lax.map(block, jnp.arange(n_blk))
    return jnp.moveaxis(o, 0, 1).reshape(bsz, n_blk * SB_BLOCK, H_B, HEAD_DIM)[:, :t]


def sb_with_cache(q, k, v, cache_k, cache_v):
    past, t = cache_k.shape[1], q.shape[1]
    k_all = jnp.concatenate([cache_k.astype(k.dtype), k], axis=1)
    v_all = jnp.concatenate([cache_v.astype(v.dtype), v], axis=1)
    return stick_breaking(q, k_all, v_all, past + jnp.arange(t), jnp.arange(past + t))


def token_mixer(u, prev_row, state0, sb_fn, w_in, mu, w0, w_up, a0, a_up, g_up, k_k, k_a,
                r_k, lnx_g, lnx_b, w_branch, w_out):
    bsz, t = u.shape[:2]
    proj = u @ w_in
    p_rwkv, q, k, v, gates = jnp.split(proj, PROJ_SPLITS, axis=-1)
    r, decay, kr, vr, kk, a, g = rwkv_inputs(p_rwkv, prev_row, mu, w0, w_up, a0, a_up, g_up, k_k, k_a)
    state, y = rwkv_scan(state0, r, decay, kr, vr, kk, a)
    o_a = rwkv_output(y, r, kr, vr, g, r_k, lnx_g, lnx_b)
    qh, kh, vh = (z.reshape(bsz, t, H_B, HEAD_DIM) for z in (q, k, v))
    o_b = sb_fn(qh, kh, vh).reshape(bsz, t, W_B)
    gate_a, gate_b = jnp.split(jax.nn.sigmoid(gates.astype(jnp.float32)), 2, axis=-1)
    merged = gate_a * (o_a @ w_branch[:W_A]) + gate_b * (o_b @ w_branch[W_A:])
    return merged @ w_out, kh, vh, state, p_rwkv[:, -1:]


def hier_moe(x, wg, bg, we, be, w_gate, w_up, w_down):
    n = x.shape[0]
    p_group = jax.nn.softmax((x @ wg + bg).astype(jnp.float32), axis=-1)
    g_idx = jnp.argmax(p_group, axis=-1)
    p_g = jnp.take_along_axis(p_group, g_idx[:, None], axis=-1)
    e_logits = (x @ we + be).reshape(n, N_GROUPS, EXPERTS_PER_GROUP)
    e_logits = jnp.take_along_axis(e_logits, g_idx[:, None, None], axis=1)[:, 0]
    top_p, top_i = lax.top_k(jax.nn.softmax(e_logits.astype(jnp.float32), axis=-1), TOP_K)
    weights = p_g * top_p / jnp.sum(top_p, axis=-1, keepdims=True)
    expert_id = g_idx[:, None] * EXPERTS_PER_GROUP + top_i
    combine = jnp.einsum('nk,nke->ne', weights, jax.nn.one_hot(expert_id, N_EXPERTS, dtype=jnp.float32))
    out = jnp.zeros((n, x.shape[1]), jnp.float32)
    for e in range(N_EXPERTS):
        h = jax.nn.silu(x @ w_gate[e]) * (x @ w_up[e])
        out = out + combine[:, e:e + 1] * (h @ w_down[e])
    return out


def setup_inputs(seed: int = 0) -> dict:
    key = jax.random.key(seed)
    ks = jax.random.split(key, 40)
    f32 = jnp.float32

    def nrm(i, shape, scale):
        return jax.random.normal(ks[i], shape, f32) * scale

    def unif(i, shape, lo, hi):
        return jax.random.uniform(ks[i], shape, f32, lo, hi)

    L, D = DEPTH, D_MODEL
    col_scale = (jnp.ones((P_COLS,), f32).at[2 * W_A:3 * W_A].multiply(BETA)
                 .at[PROJ_SPLITS[2]:PROJ_SPLITS[3]].multiply(BETA))
    return {
        'x_prompt': nrm(0, (BATCH, SEQ, D), 1.0),
        'x_sample': nrm(1, (DEC_BATCH, DEC_SEQ, D), 1.0),
        'cache_sb_k': nrm(2, (L, DEC_BATCH, PAST_LEN, H_B, HEAD_DIM), 1.0),
        'cache_sb_v': nrm(3, (L, DEC_BATCH, PAST_LEN, H_B, HEAD_DIM), BETA),
        'state_rwkv': nrm(4, (L, DEC_BATCH, H_A, HEAD_DIM, HEAD_DIM), 1.0),
        'state_rwkv_shift': nrm(5, (L, DEC_BATCH, 1, W_SHIFT), 1.0),
        'meta_tokens': nrm(6, (N_META, D), 1.0),
        'ln_in_g': 1.0 + nrm(7, (D,), 0.02),
        'ln_in_b': nrm(8, (D,), 0.02),
        'w_in': nrm(9, (L, D, P_COLS), D ** -0.5) * col_scale,
        'rwkv_mu': unif(10, (L, W_SHIFT), 0.0, 1.0),
        'rwkv_w0': unif(11, (L, W_A), -6.0, 1.0),
        'rwkv_w_up': nrm(12, (L, R_DECAY, W_A), 0.1),
        'rwkv_a0': nrm(13, (L, W_A), 0.1),
        'rwkv_a_up': nrm(14, (L, R_A, W_A), R_A ** -0.5),
        'rwkv_g_up': nrm(15, (L, R_GATE, W_A), R_GATE ** -0.5),
        'rwkv_k_k': 0.85 + nrm(16, (L, W_A), 0.02),
        'rwkv_k_a': 1.0 + nrm(17, (L, W_A), 0.02),
        'rwkv_r_k': nrm(18, (L, H_A, HEAD_DIM), 0.1),
        'rwkv_lnx_g': 1.0 + nrm(19, (L, W_A), 0.02),
        'rwkv_lnx_b': nrm(20, (L, W_A), 0.02),
        'w_branch': nrm(21, (L, MIX_WIDTH, D), MIX_WIDTH ** -0.5 * BETA),
        'w_out': nrm(22, (L, D, D), D ** -0.5 * BETA),
        'ln_mix_g': 1.0 + nrm(23, (L, D), 0.02),
        'ln_mix_b': nrm(24, (L, D), 0.02),
        'router_group_w': nrm(25, (L, D, N_GROUPS), D ** -0.5),
        'router_group_b': nrm(26, (L, N_GROUPS), 0.01),
        'router_expert_w': nrm(27, (L, D, N_EXPERTS), D ** -0.5),
        'router_expert_b': nrm(28, (L, N_EXPERTS), 0.01),
        'moe_w_gate': nrm(29, (L, N_EXPERTS, D, D_EXPERT), D ** -0.5),
        'moe_w_up': nrm(30, (L, N_EXPERTS, D, D_EXPERT), D ** -0.5),
        'moe_w_down': nrm(31, (L, N_EXPERTS, D_EXPERT, D), D_EXPERT ** -0.5 * BETA),
        'ln_ffn_g': 1.0 + nrm(32, (L, D), 0.02),
        'ln_ffn_b': nrm(33, (L, D), 0.02),
    }


def reference(x_prompt, x_sample, cache_sb_k, cache_sb_v, state_rwkv, state_rwkv_shift,
              meta_tokens, ln_in_g, ln_in_b, w_in, rwkv_mu, rwkv_w0, rwkv_w_up, rwkv_a0,
              rwkv_a_up, rwkv_g_up, rwkv_k_k, rwkv_k_a, rwkv_r_k, rwkv_lnx_g, rwkv_lnx_b,
              w_branch, w_out, ln_mix_g, ln_mix_b, router_group_w, router_group_b,
              router_expert_w, router_expert_b, moe_w_gate, moe_w_up, moe_w_down,
              ln_ffn_g, ln_ffn_b):
    bp = x_prompt.shape[0]
    bs, ts = x_sample.shape[:2]
    meta = jnp.broadcast_to(meta_tokens.astype(x_prompt.dtype)[None], (bp, N_META, D_MODEL))
    h_p = layer_norm(jnp.concatenate([meta, x_prompt], axis=1), ln_in_g, ln_in_b)
    h_s = layer_norm(x_sample, ln_in_g, ln_in_b)
    kp_l, vp_l, sp_l, shp_l, ks_l, vs_l, ss_l, shs_l = [], [], [], [], [], [], [], []
    for l in range(DEPTH):
        rw = (rwkv_mu[l], rwkv_w0[l], rwkv_w_up[l], rwkv_a0[l], rwkv_a_up[l], rwkv_g_up[l],
              rwkv_k_k[l], rwkv_k_a[l], rwkv_r_k[l], rwkv_lnx_g[l], rwkv_lnx_b[l])
        zero_prev = jnp.zeros((bp, 1, W_SHIFT), jnp.float32)
        zero_state = jnp.zeros((bp, H_A, HEAD_DIM, HEAD_DIM), jnp.float32)
        mix_p, k_p, v_p, st_p, sh_p = token_mixer(h_p, zero_prev, zero_state, sb_prompt,
                                                  w_in[l], *rw, w_branch[l], w_out[l])
        sb_cached = functools.partial(sb_with_cache, cache_k=cache_sb_k[l], cache_v=cache_sb_v[l])
        mix_s, k_s, v_s, st_s, sh_s = token_mixer(h_s, state_rwkv_shift[l], state_rwkv[l], sb_cached,
                                                  w_in[l], *rw, w_branch[l], w_out[l])
        h_p = layer_norm(ALPHA * h_p + mix_p, ln_mix_g[l], ln_mix_b[l])
        h_s = layer_norm(ALPHA * h_s + mix_s, ln_mix_g[l], ln_mix_b[l])
        if l == DEPTH - 1:
            h_p = h_p[:, N_META:]
        tp = h_p.shape[1]
        flat = jnp.concatenate([h_p.reshape(-1, D_MODEL), h_s.reshape(-1, D_MODEL)], axis=0)
        ffn = hier_moe(flat, router_group_w[l], router_group_b[l], router_expert_w[l],
                       router_expert_b[l], moe_w_gate[l], moe_w_up[l], moe_w_down[l])
        flat = layer_norm(ALPHA * flat + ffn, ln_ffn_g[l], ln_ffn_b[l])
        h_p = flat[:bp * tp].reshape(bp, tp, D_MODEL)
        h_s = flat[bp * tp:].reshape(bs, ts, D_MODEL)
        kp_l.append(k_p); vp_l.append(v_p); sp_l.append(st_p); shp_l.append(sh_p)
        ks_l.append(k_s); vs_l.append(v_s); ss_l.append(st_s); shs_l.append(sh_s)
    return (h_p, h_s, jnp.stack(kp_l), jnp.stack(vp_l), jnp.stack(sp_l), jnp.stack(shp_l),
            jnp.stack(ks_l), jnp.stack(vs_l), jnp.stack(ss_l), jnp.stack(shs_l))
```

```python
import functools
import math

import jax
import jax.numpy as jnp
from jax import lax
from jax.experimental import pallas as pl
from jax.experimental.pallas import tpu as pltpu

F32 = jnp.float32
BF16 = jnp.bfloat16

HEAD_DIM = 64
LANES = 128
CHUNK = 64
TOP_K = 2
LN_EPS = 1e-5
GN_EPS = 64e-5
VMEM_LIMIT = 56 * 1024 * 1024


def _round_up(n, m):
    return -(-n // m) * m


def _pick_tile(n, candidates):
    for c in candidates:
        if n % c == 0:
            return c
    raise ValueError(f"no tile in {candidates} divides {n}")


def _params(sem, vmem=VMEM_LIMIT):
    return pltpu.CompilerParams(dimension_semantics=sem, vmem_limit_bytes=vmem)


def _dot(a, b):
    return jnp.dot(a.astype(BF16), b.astype(BF16), preferred_element_type=F32)


def _split2(x):
    hi = x.astype(BF16)
    lo = (x - hi.astype(F32)).astype(BF16)
    return hi, lo


def _split3(x):
    hi = x.astype(BF16)
    r1 = x - hi.astype(F32)
    mid = r1.astype(BF16)
    lo = (r1 - mid.astype(F32)).astype(BF16)
    return hi, mid, lo


def _dot3(a, b):
    ah, al = _split2(a)
    bh, bl = _split2(b)
    d = lambda x, y: jnp.dot(x, y, preferred_element_type=F32)
    return d(ah, bh) + (d(ah, bl) + d(al, bh))


def _dot3_nt(a, b):
    ah, al = _split2(a)
    bh, bl = _split2(b)
    d = lambda x, y: lax.dot_general(x, y, (((1,), (1,)), ((), ())), preferred_element_type=F32)
    return d(ah, bh) + (d(ah, bl) + d(al, bh))


def _dot_exact_rhs(a, b_bf16):
    hi, mid, lo = _split3(a)
    d = lambda x: jnp.dot(x, b_bf16, preferred_element_type=F32)
    return d(hi) + (d(mid) + d(lo))


def _dot_exact_lhs(a_bf16, b):
    hi, mid, lo = _split3(b)
    d = lambda x: jnp.dot(a_bf16, x, preferred_element_type=F32)
    return d(hi) + (d(mid) + d(lo))


def _layer_norm(x, g, b):
    mu = jnp.mean(x, axis=-1, keepdims=True)
    xc = x - mu
    var = jnp.mean(xc * xc, axis=-1, keepdims=True)
    return xc * lax.rsqrt(var + LN_EPS) * g + b


def _sigmoid(x):
    return 1.0 / (1.0 + jnp.exp(-x))


def _softplus(x):
    return jnp.maximum(x, 0.0) + jnp.log(1.0 + jnp.exp(-jnp.abs(x)))


def _iota(shape, dim):
    return lax.broadcasted_iota(jnp.int32, shape, dim)


def _inproj_kernel(x_ref, g_ref, b_ref, w_ref, o_ref, xn_ref):
    @pl.when(pl.program_id(1) == 0)
    def _():
        xn_ref[...] = _layer_norm(x_ref[...], g_ref[...], b_ref[...]).astype(BF16)

    o_ref[...] = jnp.dot(xn_ref[...], w_ref[...], preferred_element_type=F32)


def _inproj(x_all, ln_g, ln_b, w_pad, tm, tn):
    rows, d = x_all.shape
    cols = w_pad.shape[1]
    return pl.pallas_call(
        _inproj_kernel,
        grid=(rows // tm, cols // tn),
        in_specs=[
            pl.BlockSpec((tm, d), lambda i, j: (i, 0)),
            pl.BlockSpec((1, d), lambda i, j: (0, 0)),
            pl.BlockSpec((1, d), lambda i, j: (0, 0)),
            pl.BlockSpec((d, tn), lambda i, j: (0, j)),
        ],
        out_specs=pl.BlockSpec((tm, tn), lambda i, j: (i, j)),
        out_shape=jax.ShapeDtypeStruct((rows, cols), F32),
        scratch_shapes=[pltpu.VMEM((tm, d), BF16)],
        compiler_params=_params(("parallel", "arbitrary")),
        name="inproj",
    )(x_all, ln_g, ln_b, w_pad)


def _rwkv_kernel(p_ref, head_ref, prev_ref, h0_ref, mu_ref, w0_ref, a0_ref, kk_ref, ka_ref, rk_ref,
                 lng_ref, lnb_ref, wup_ref, aup_ref, gup_ref,
                 o_ref, hout_ref, carry_ref, hbd_ref, *, n_pairs, has_head, n_null, dw, da, dg):
    t = pl.program_id(1)
    nt = pl.num_programs(1)
    w_a = n_pairs * LANES
    c = CHUNK

    @pl.when(t == 0)
    def _():
        carry_ref[...] = prev_ref[0]
        hbd_ref[...] = h0_ref[0]

    if has_head:
        p = jnp.where(t == 0, head_ref[...], p_ref[...])
    else:
        p = p_ref[...]
    row = _iota((c, 1), 0)
    shifted = jnp.where(row == 0, carry_ref[...], pltpu.roll(p, 1, axis=0))
    carry_ref[...] = p[c - 1:c, :]
    xm = p + (shifted - p) * mu_ref[...]

    r = xm[:, 0:w_a]
    k = xm[:, w_a:2 * w_a]
    v = xm[:, 2 * w_a:3 * w_a]
    o = 3 * w_a
    xw = xm[:, o:o + dw]
    xa = xm[:, o + dw:o + dw + da]
    xg = xm[:, o + dw + da:o + dw + da + dg]

    w = -_softplus(-(w0_ref[...] + _dot3(jnp.tanh(xw), wup_ref[...]))) - 0.5
    lw = -jnp.exp(w)
    a = _sigmoid(a0_ref[...] + _dot3(xa, aup_ref[...]))
    g = _dot3(_sigmoid(xg), gup_ref[...])
    kk0 = k * kk_ref[...]
    kmod = k * (1.0 + (a - 1.0) * ka_ref[...])
    if has_head:
        valid = jnp.logical_or(row >= n_null, t > 0)
        lw = jnp.where(valid, lw, 0.0)
        kk0 = jnp.where(valid, kk0, 0.0)
        kmod = jnp.where(valid, kmod, 0.0)

    tri = (_iota((c, c), 1) <= _iota((c, c), 0)).astype(BF16)
    lcum = _dot_exact_lhs(tri, lw)
    lend = lcum[c - 1:c, :]
    e_pos = jnp.exp(lcum)
    e_prev = jnp.exp(lcum - lw)
    e_neg = jnp.exp(-lcum)
    e_end = jnp.exp(lend - lcum)

    lane = _iota((c, LANES), 1)
    rowc = _iota((c, LANES), 0)
    first = lane < HEAD_DIM
    col = jnp.where(first, lane, lane - HEAD_DIM)
    strict = col < rowc
    incl = col <= rowc
    eye = (col == rowc).astype(F32)
    rr = _iota((LANES, LANES), 0)
    cc = _iota((LANES, LANES), 1)
    bd_mask = (rr < HEAD_DIM) == (cc < HEAD_DIM)
    ones_bd = bd_mask.astype(BF16)
    avg_bd = (bd_mask.astype(F32) * (1.0 / HEAD_DIM)).astype(BF16)
    ones_cl = jnp.ones((c, LANES), BF16)

    def bd(x):
        return jnp.concatenate([jnp.where(first, x, 0.0), jnp.where(first, 0.0, x)], axis=0)

    def pack(x1, x2, left):
        if left:
            return jnp.where(first, x1, pltpu.roll(x2, HEAD_DIM, axis=1))
        return jnp.where(first, pltpu.roll(x1, HEAD_DIM, axis=1), x2)

    for pr in range(n_pairs):
        sl = slice(pr * LANES, (pr + 1) * LANES)
        kk0p = kk0[:, sl]
        ss = _dot_exact_rhs(kk0p * kk0p, ones_bd)
        kkn = kk0p / jnp.maximum(jnp.sqrt(ss), 1e-12)
        bb = kkn * a[:, sl]
        a_t = -kkn * e_prev[:, sl]
        b_t = bb * e_neg[:, sl]
        k_t = kmod[:, sl] * e_neg[:, sl]
        r_t = r[:, sl] * e_pos[:, sl]
        b_h = bb * e_end[:, sl]
        k_h = kmod[:, sl] * e_end[:, sl]
        vp = v[:, sl]

        lhs = jnp.concatenate([jnp.where(first, a_t, 0.0), jnp.where(first, 0.0, a_t),
                               jnp.where(first, r_t, 0.0), jnp.where(first, 0.0, r_t)], axis=0)
        rhs = jnp.concatenate([b_t, k_t], axis=0)
        s = _dot3_nt(lhs, rhs)
        x1, x2, x3, x4 = s[0:c], s[c:2 * c], s[2 * c:3 * c], s[3 * c:4 * c]
        m_ab = jnp.where(strict, pack(x1, x2, True), 0.0)
        m_ak = jnp.where(strict, pack(x1, x2, False), 0.0)
        m_rb = jnp.where(incl, pack(x3, x4, True), 0.0)
        m_rk = jnp.where(incl, pack(x3, x4, False), 0.0)

        pw = m_ab
        tinv = eye + m_ab
        for _ in range(5):
            pw = _dot3(pw, bd(pw))
            tinv = tinv + _dot3(tinv, bd(pw))

        h = hbd_ref[pr]
        vbd = bd(vp)
        wm = _dot3(a_t, h) + _dot3(m_ak, vbd)
        u = _dot3(tinv, bd(wm))
        y = _dot3(r_t, h) + _dot3(m_rb, bd(u)) + _dot3(m_rk, vbd)

        lend_col = _dot_exact_rhs(lw[:, sl].T, ones_cl)
        upd = _dot3(jnp.concatenate([b_h, k_h], axis=0).T, jnp.concatenate([u, vp], axis=0))
        hbd_ref[pr] = jnp.where(bd_mask, h * jnp.exp(lend_col) + upd, 0.0)

        mean = _dot_exact_rhs(y, avg_bd)
        yc = y - mean
        var = _dot_exact_rhs(yc * yc, avg_bd)
        yn = yc * lax.rsqrt(var + GN_EPS) * lng_ref[:, sl] + lnb_ref[:, sl]
        bonus = _dot_exact_rhs(r[:, sl] * kmod[:, sl] * rk_ref[:, sl], ones_bd) * vp
        o_ref[:, sl] = ((yn + bonus) * g[:, sl]).astype(o_ref.dtype)

    @pl.when(t == nt - 1)
    def _():
        hout_ref[0] = hbd_ref[...]


def _rwkv(proj, head, prev, h0, rw, *, n_seq, n_tiles, row_block, out_block, out_rows, has_head, n_null, pr_cols,
          widths):
    n_pairs = h0.shape[1]
    w_a = n_pairs * LANES
    dw, da, dg = widths
    vec = lambda n: pl.BlockSpec((1, n), lambda s, t: (0, 0))
    mat = lambda a: pl.BlockSpec(a.shape, lambda s, t: (0, 0))
    kern = functools.partial(_rwkv_kernel, n_pairs=n_pairs, has_head=has_head, n_null=n_null,
                             dw=dw, da=da, dg=dg)
    return pl.pallas_call(
        kern,
        grid=(n_seq, n_tiles),
        in_specs=[
            pl.BlockSpec((CHUNK, pr_cols), lambda s, t: (row_block(s, t), 0)),
            pl.BlockSpec((CHUNK, pr_cols), lambda s, t: (0, 0)),
            pl.BlockSpec((1, 1, pr_cols), lambda s, t: (s, 0, 0)),
            pl.BlockSpec((1, n_pairs, LANES, LANES), lambda s, t: (s, 0, 0, 0)),
            vec(pr_cols), vec(w_a), vec(w_a), vec(w_a), vec(w_a), vec(w_a), vec(w_a), vec(w_a),
            mat(rw["w_up"]), mat(rw["a_up"]), mat(rw["g_up"]),
        ],
        out_specs=[
            pl.BlockSpec((CHUNK, w_a), lambda s, t: (out_block(s, t), 0)),
            pl.BlockSpec((1, n_pairs, LANES, LANES), lambda s, t: (s, 0, 0, 0)),
        ],
        out_shape=[
            jax.ShapeDtypeStruct((out_rows, w_a), BF16),
            jax.ShapeDtypeStruct((n_seq, n_pairs, LANES, LANES), F32),
        ],
        scratch_shapes=[pltpu.VMEM((1, pr_cols), F32), pltpu.VMEM((n_pairs, LANES, LANES), F32)],
        compiler_params=_params(("parallel", "arbitrary")),
        name="rwkv_head" if has_head else "rwkv",
    )(proj, head, prev, h0, rw["mu"], rw["w0"], rw["a0"], rw["k_k"], rw["k_a"], rw["r_k"],
      rw["lnx_g"], rw["lnx_b"], rw["w_up"], rw["a_up"], rw["g_up"])


def _sb_kernel(q_ref, k_ref, v_ref, o_ref, kt_ref, vb_ref, acc_ref, c_ref, *, bq, bk, n_null, q_tile0, scale):
    qi = pl.program_id(2)
    n_kb = kt_ref.shape[0]

    @pl.when(qi == 0)
    def _():
        for j in range(n_kb):
            for s in range(bk // LANES):
                blk = k_ref[0, j * bk + s * LANES:j * bk + (s + 1) * LANES, :]
                kt_ref[j, :, s * LANES:(s + 1) * LANES] = blk.T.astype(BF16)
        vb_ref[...] = v_ref[0].astype(BF16)

    i = qi + q_tile0
    j_max = (i * bq) // bk
    lane_q = _iota((bq, LANES), 1)
    later = (_iota((bk, bk), 0) > _iota((bk, bk), 1)).astype(BF16)
    ones_k = jnp.ones((bk, LANES), BF16)
    q = q_ref[...] * scale

    for hh in range(2):
        in_head = (lane_q < HEAD_DIM) if hh == 0 else (lane_q >= HEAD_DIM)
        qm = jnp.where(in_head, q, 0.0).astype(BF16)
        acc_ref[...] = jnp.zeros_like(acc_ref)
        c_ref[...] = jnp.zeros_like(c_ref)

        def block(j, masked):
            z = jnp.dot(qm, kt_ref[j], preferred_element_type=F32)
            l1p = jnp.log(1.0 + jnp.exp(-jnp.abs(z)))
            log_beta = jnp.minimum(z, 0.0) - l1p
            log_rest = -(jnp.maximum(z, 0.0) + l1p)
            if masked:
                kpos = j * bk + _iota((bq, bk), 1)
                qpos = i * bq + _iota((bq, bk), 0)
                vis = jnp.logical_and(kpos < qpos, kpos >= n_null)
                log_rest = jnp.where(vis, log_rest, 0.0)
            hi, lo = _split2(log_rest)
            d = lambda x, m: jnp.dot(x, m, preferred_element_type=F32)
            between = d(hi, later) + d(lo, later)
            total = d(hi, ones_k) + d(lo, ones_k)
            cb = c_ref[...]
            wgt = jnp.exp(log_beta + between + jnp.concatenate([cb] * (bk // LANES), axis=1))
            if masked:
                wgt = jnp.where(vis, wgt, 0.0)
            vblk = vb_ref[pl.ds(pl.multiple_of(j * bk, bk), bk), :]
            acc_ref[...] += jnp.dot(wgt.astype(BF16), vblk, preferred_element_type=F32)
            c_ref[...] = cb + total

        block(j_max, True)

        def body(n, carry):
            block(j_max - 1 - n, False)
            return carry

        lax.fori_loop(0, jnp.maximum(j_max - 1, 0), body, 0)

        @pl.when(j_max > 0)
        def _():
            block(0, True)

        if hh == 0:
            o_ref[...] = acc_ref[...].astype(o_ref.dtype)
        else:
            o_ref[...] = jnp.where(in_head, acc_ref[...].astype(o_ref.dtype), o_ref[...])


def _sb(q_arr, q_block, k_seq, v_seq, *, n_qtiles, out_rows, out_block, bq, bk, n_null, q_tile0):
    nb, tk, w_b = k_seq.shape
    n_pairs = w_b // LANES
    kern = functools.partial(_sb_kernel, bq=bq, bk=bk, n_null=n_null, q_tile0=q_tile0,
                             scale=HEAD_DIM ** -0.5)
    return pl.pallas_call(
        kern,
        grid=(nb, n_pairs, n_qtiles),
        in_specs=[
            pl.BlockSpec((bq, LANES), q_block),
            pl.BlockSpec((1, tk, LANES), lambda b, p, i: (b, 0, p)),
            pl.BlockSpec((1, tk, LANES), lambda b, p, i: (b, 0, p)),
        ],
        out_specs=pl.BlockSpec((bq, LANES), out_block),
        out_shape=jax.ShapeDtypeStruct((out_rows, w_b), BF16),
        scratch_shapes=[
            pltpu.VMEM((tk // bk, LANES, bk), BF16),
            pltpu.VMEM((tk, LANES), BF16),
            pltpu.VMEM((bq, LANES), F32),
            pltpu.VMEM((bq, LANES), F32),
        ],
        compiler_params=_params(("parallel", "parallel", "arbitrary")),
        name="stick_breaking",
    )(q_arr, k_seq, v_seq)


def _mix_kernel(*refs, n_gc, gw, w_a, alpha, n_exp, n_grp):
    x_ref, oa_ref, ob_ref = refs[0:3]
    ga_refs = refs[3:3 + n_gc]
    gb_refs = refs[3 + n_gc:3 + 2 * n_gc]
    (wb_ref, wo_ref, lig_ref, lib_ref, lmg_ref, lmb_ref, wr_ref, br_ref,
     h_ref, comb_ref, mg_ref) = refs[3 + 2 * n_gc:]
    oa = oa_ref[...]
    ob = ob_ref[...]
    for cidx in range(n_gc):
        cs = slice(cidx * gw, (cidx + 1) * gw)
        pa = jnp.dot(oa, wb_ref[0:w_a, cs], preferred_element_type=F32)
        pb = jnp.dot(ob, wb_ref[w_a:, cs], preferred_element_type=F32)
        mg_ref[:, cs] = (_sigmoid(ga_refs[cidx][...]) * pa + _sigmoid(gb_refs[cidx][...]) * pb).astype(BF16)
    mix = jnp.dot(mg_ref[...], wo_ref[...], preferred_element_type=F32)
    h = _layer_norm(x_ref[...], lig_ref[...], lib_ref[...])
    h1 = _layer_norm(alpha * h + mix, lmg_ref[...], lmb_ref[...])
    h_ref[...] = h1

    lg = _dot3(h1, wr_ref[...]) + br_ref[...]
    lane = _iota(lg.shape, 1).astype(F32)
    neg, big = -1e30, 1e9
    epg = n_exp // n_grp
    is_g = jnp.logical_and(lane >= n_exp, lane < n_exp + n_grp)
    gmax = jnp.max(jnp.where(is_g, lg, neg), axis=1, keepdims=True)
    gsum = jnp.sum(jnp.where(is_g, jnp.exp(jnp.where(is_g, lg, neg) - gmax), 0.0), axis=1, keepdims=True)
    p_g = 1.0 / gsum
    g_idx = jnp.min(jnp.where(jnp.logical_and(is_g, lg == gmax), lane - n_exp, big), axis=1, keepdims=True)
    in_grp = jnp.logical_and(lane >= g_idx * epg, lane < (g_idx + 1.0) * epg)
    el = jnp.where(in_grp, lg, neg)
    m1 = jnp.max(el, axis=1, keepdims=True)
    i1 = jnp.min(jnp.where(jnp.logical_and(in_grp, lg == m1), lane, big), axis=1, keepdims=True)
    el2 = jnp.where(lane == i1, neg, el)
    m2 = jnp.max(el2, axis=1, keepdims=True)
    i2 = jnp.min(jnp.where(jnp.logical_and(el2 == m2, in_grp), lane, big), axis=1, keepdims=True)
    e2 = jnp.exp(m2 - m1)
    w1 = p_g / (1.0 + e2)
    w2 = p_g * e2 / (1.0 + e2)
    comb_ref[...] = jnp.where(lane == i1, w1, 0.0) + jnp.where(lane == i2, w2, 0.0)


def _mix(x_all, o_a, o_b, proj, gate_off, wb, wo, ln_in, ln_mix, wr, br, *, rows, tm, alpha, n_exp, n_grp):
    d = x_all.shape[1]
    w_a = o_a.shape[1]
    gw = math.gcd(gate_off, d)
    n_gc = d // gw
    gspec = lambda blk: pl.BlockSpec((tm, gw), lambda i, blk=blk: (i, blk))
    const = lambda a: pl.BlockSpec(a.shape, lambda i: (0,) * a.ndim, pipeline_mode=pl.Buffered(1))
    kern = functools.partial(_mix_kernel, n_gc=n_gc, gw=gw, w_a=w_a, alpha=alpha, n_exp=n_exp, n_grp=n_grp)
    g0 = gate_off // gw
    args = [x_all, o_a, o_b] + [proj] * (2 * n_gc) + [wb, wo, ln_in[0], ln_in[1], ln_mix[0], ln_mix[1], wr, br]
    in_specs = ([pl.BlockSpec((tm, d), lambda i: (i, 0)),
                 pl.BlockSpec((tm, w_a), lambda i: (i, 0)),
                 pl.BlockSpec((tm, o_b.shape[1]), lambda i: (i, 0))]
                + [gspec(g0 + cidx) for cidx in range(n_gc)]
                + [gspec(g0 + n_gc + cidx) for cidx in range(n_gc)]
                + [const(a) for a in args[3 + 2 * n_gc:]])
    return pl.pallas_call(
        kern,
        grid=(rows // tm,),
        in_specs=in_specs,
        out_specs=[pl.BlockSpec((tm, d), lambda i: (i, 0)), pl.BlockSpec((tm, LANES), lambda i: (i, 0))],
        out_shape=[jax.ShapeDtypeStruct((rows, d), F32), jax.ShapeDtypeStruct((rows, LANES), F32)],
        scratch_shapes=[pltpu.VMEM((tm, d), BF16)],
        compiler_params=_params(("parallel",)),
        name="mix_out_router",
    )(*args)


def _moe_kernel(x_ref, c_ref, wg_ref, wu_ref, wd_ref, g_ref, b_ref, o_ref, xb_ref, acc_ref, *, alpha):
    e = pl.program_id(1)

    @pl.when(e == 0)
    def _():
        xb_ref[...] = x_ref[...].astype(BF16)
        acc_ref[...] = jnp.zeros_like(acc_ref)

    xb = xb_ref[...]
    hg = jnp.dot(xb, wg_ref[0], preferred_element_type=F32)
    hu = jnp.dot(xb, wu_ref[0], preferred_element_type=F32)
    cmb = c_ref[...]
    ce = jnp.sum(jnp.where(_iota(cmb.shape, 1) == e, cmb, 0.0), axis=1, keepdims=True)
    h = hg * _sigmoid(hg) * hu * ce
    acc_ref[...] += jnp.dot(h.astype(BF16), wd_ref[0], preferred_element_type=F32)

    @pl.when(e == pl.num_programs(1) - 1)
    def _():
        o_ref[...] = _layer_norm(alpha * x_ref[...] + acc_ref[...], g_ref[...], b_ref[...])


def _moe(h1, comb, wg, wu, wd, ln_g, ln_b, *, tm, alpha):
    rows, d = h1.shape
    n_exp, _, d_e = wg.shape
    return pl.pallas_call(
        functools.partial(_moe_kernel, alpha=alpha),
        grid=(rows // tm, n_exp),
        in_specs=[
            pl.BlockSpec((tm, d), lambda i, e: (i, 0)),
            pl.BlockSpec((tm, LANES), lambda i, e: (i, 0)),
            pl.BlockSpec((1, d, d_e), lambda i, e: (e, 0, 0)),
            pl.BlockSpec((1, d, d_e), lambda i, e: (e, 0, 0)),
            pl.BlockSpec((1, d_e, d), lambda i, e: (e, 0, 0)),
            pl.BlockSpec((1, d), lambda i, e: (0, 0)),
            pl.BlockSpec((1, d), lambda i, e: (0, 0)),
        ],
        out_specs=pl.BlockSpec((tm, d), lambda i, e: (i, 0)),
        out_shape=jax.ShapeDtypeStruct((rows, d), F32),
        scratch_shapes=[pltpu.VMEM((tm, d), BF16), pltpu.VMEM((tm, d), F32)],
        compiler_params=_params(("parallel", "arbitrary")),
        name="moe",
    )(h1, comb, wg, wu, wd, ln_g, ln_b)


def _pad_cols(a, n):
    return jnp.pad(a, [(0, 0)] * (a.ndim - 1) + [(0, n - a.shape[-1])])


def kernel(x_prompt, x_sample, cache_sb_k, cache_sb_v, state_rwkv, state_rwkv_shift, meta_tokens, ln_in_g, ln_in_b, w_in, rwkv_mu, rwkv_w0, rwkv_w_up, rwkv_a0, rwkv_a_up, rwkv_g_up, rwkv_k_k, rwkv_k_a, rwkv_r_k, rwkv_lnx_g, rwkv_lnx_b, w_branch, w_out, ln_mix_g, ln_mix_b, router_group_w, router_group_b, router_expert_w, router_expert_b, moe_w_gate, moe_w_up, moe_w_down, ln_ffn_g, ln_ffn_b):
    depth = w_in.shape[0]
    assert depth == 1, "single-layer step only"
    nb, seq, d = x_prompt.shape
    db, ds = x_sample.shape[:2]
    n_meta = meta_tokens.shape[0]
    past = cache_sb_k.shape[2]
    w_a = rwkv_w0.shape[1]
    r_d, r_a, r_g = rwkv_w_up.shape[1], rwkv_a_up.shape[1], rwkv_g_up.shape[1]
    w_shift = 3 * w_a + r_d + r_a + r_g
    w_b = (w_in.shape[2] - w_shift - 2 * d) // 3
    h_a, h_b = w_a // HEAD_DIM, w_b // HEAD_DIM
    n_grp = router_group_w.shape[2]
    n_exp = router_expert_w.shape[2]
    alpha = (2 * depth) ** 0.25
    assert w_a % LANES == 0 and w_b % LANES == 0 and seq % 256 == 0 and ds == CHUNK and n_meta <= CHUNK
    assert n_exp + n_grp <= LANES

    dw, da, dg = (_round_up(n, LANES) for n in (r_d, r_a, r_g))
    pr_cols = 3 * w_a + dw + da + dg
    wi = w_in[0]
    s0 = 3 * w_a
    segs = [wi[:, :s0], _pad_cols(wi[:, s0:s0 + r_d], dw), _pad_cols(wi[:, s0 + r_d:s0 + r_d + r_a], da),
            _pad_cols(wi[:, s0 + r_d + r_a:w_shift], dg), wi[:, w_shift:]]
    w_pad = jnp.concatenate(segs, axis=1).astype(BF16)
    p_cols = w_pad.shape[1]
    q_off, k_off, v_off, gate_off = pr_cols, pr_cols + w_b, pr_cols + 2 * w_b, pr_cols + 3 * w_b

    def to_padded_rwkv_cols(a):
        return jnp.concatenate([a[..., :s0], _pad_cols(a[..., s0:s0 + r_d], dw),
                                _pad_cols(a[..., s0 + r_d:s0 + r_d + r_a], da),
                                _pad_cols(a[..., s0 + r_d + r_a:], dg)], axis=-1)

    def from_padded_rwkv_cols(a):
        return jnp.concatenate([a[..., :s0], a[..., s0:s0 + r_d], a[..., s0 + dw:s0 + dw + r_a],
                                a[..., s0 + dw + da:s0 + dw + da + r_g]], axis=-1)

    n_p, n_s = nb * seq, db * ds
    rows_main = n_p + n_s
    tm1 = _pick_tile(rows_main, (512, 256, 128))
    rows_all = _round_up(rows_main + n_meta, tm1)
    x_all = jnp.concatenate([x_prompt.reshape(n_p, d), x_sample.reshape(n_s, d), meta_tokens.astype(F32),
                             jnp.zeros((rows_all - rows_main - n_meta, d), F32)], axis=0)
    tn1 = _pick_tile(p_cols, (512, 256, 128))
    proj = _inproj(x_all, ln_in_g[None], ln_in_b[None], w_pad, tm1, tn1)
    proj_meta = proj[rows_main:rows_main + n_meta]

    row2 = lambda a: a.reshape(1, -1)
    rw = dict(mu=to_padded_rwkv_cols(rwkv_mu[0])[None], w0=row2(rwkv_w0[0]), a0=row2(rwkv_a0[0]),
              k_k=row2(rwkv_k_k[0]), k_a=row2(rwkv_k_a[0]), r_k=row2(rwkv_r_k[0]),
              lnx_g=row2(rwkv_lnx_g[0]), lnx_b=row2(rwkv_lnx_b[0]),
              w_up=jnp.pad(rwkv_w_up[0], ((0, dw - r_d), (0, 0))),
              a_up=jnp.pad(rwkv_a_up[0], ((0, da - r_a), (0, 0))),
              g_up=jnp.pad(rwkv_g_up[0], ((0, dg - r_g), (0, 0))))
    n_pairs_a = w_a // LANES
    n_null = CHUNK - n_meta
    head = jnp.concatenate([jnp.zeros((n_null, pr_cols), F32), proj_meta[:, :pr_cols]], axis=0)
    tiles_p = seq // CHUNK
    o_a_p, hbd_p = _rwkv(proj, head, jnp.zeros((nb, 1, pr_cols), F32),
                         jnp.zeros((nb, n_pairs_a, LANES, LANES), F32), rw,
                         n_seq=nb, n_tiles=tiles_p + 1,
                         row_block=lambda s, t: s * tiles_p + jnp.maximum(t - 1, 0),
                         out_block=lambda s, t: s * tiles_p + jnp.maximum(t - 1, 0), out_rows=n_p, has_head=True, n_null=n_null, pr_cols=pr_cols, widths=(dw, da, dg))

    def to_hbd(state):
        n = state.shape[0]
        hkv = jnp.swapaxes(state, -1, -2).reshape(n, n_pairs_a, 2, HEAD_DIM, HEAD_DIM)
        z = jnp.zeros_like(hkv[:, :, 0])
        top = jnp.concatenate([hkv[:, :, 0], z], axis=-1)
        bot = jnp.concatenate([z, hkv[:, :, 1]], axis=-1)
        return jnp.concatenate([top, bot], axis=-2)

    def from_hbd(hbd):
        h0 = hbd[:, :, :HEAD_DIM, :HEAD_DIM]
        h1 = hbd[:, :, HEAD_DIM:, HEAD_DIM:]
        hkv = jnp.stack([h0, h1], axis=2).reshape(hbd.shape[0], h_a, HEAD_DIM, HEAD_DIM)
        return jnp.swapaxes(hkv, -1, -2)

    base_s = n_p // CHUNK
    o_a_s, hbd_s = _rwkv(proj, head, to_padded_rwkv_cols(state_rwkv_shift[0]), to_hbd(state_rwkv[0]), rw,
                         n_seq=db, n_tiles=ds // CHUNK,
                         row_block=lambda s, t: base_s + s * (ds // CHUNK) + t,
                         out_block=lambda s, t: s * (ds // CHUNK) + t, out_rows=n_s, has_head=False, n_null=0, pr_cols=pr_cols, widths=(dw, da, dg))
    o_a = jnp.concatenate([o_a_p, o_a_s], axis=0)

    bq, bk = 128, 256
    seg = lambda off, r0, r1: proj[r0:r1, off:off + w_b]
    k_meta, v_meta = proj_meta[:, k_off:k_off + w_b], proj_meta[:, v_off:v_off + w_b]
    k_main = seg(k_off, 0, n_p).reshape(nb, seq, w_b)
    v_main = seg(v_off, 0, n_p).reshape(nb, seq, w_b)
    null_p = bk - n_meta
    front = lambda m: jnp.concatenate([jnp.zeros((nb, null_p, w_b), F32),
                                       jnp.broadcast_to(m[None], (nb, n_meta, w_b))], axis=1)
    k_seq_p = jnp.concatenate([front(k_meta), k_main], axis=1)
    v_seq_p = jnp.concatenate([front(v_meta), v_main], axis=1)
    qt_p = seq // bq
    o_b_p = _sb(proj, lambda b, p, i: (b * qt_p + i, q_off // LANES + p), k_seq_p, v_seq_p,
                n_qtiles=qt_p, out_rows=n_p, out_block=lambda b, p, i: (b * qt_p + i, p),
                bq=bq, bk=bk, n_null=null_p, q_tile0=bk // bq)

    k_new = seg(k_off, n_p, rows_main).reshape(db, ds, w_b)
    v_new = seg(v_off, n_p, rows_main).reshape(db, ds, w_b)
    q_new = seg(q_off, n_p, rows_main).reshape(db, ds, w_b)
    tk_s = _round_up(past + ds, bk)
    null_s = tk_s - past - ds
    assert ds <= bq and null_s < bk
    zs = jnp.zeros((db, null_s, w_b), F32)
    k_seq_s = jnp.concatenate([zs, cache_sb_k[0].reshape(db, past, w_b), k_new], axis=1)
    v_seq_s = jnp.concatenate([zs, cache_sb_v[0].reshape(db, past, w_b), v_new], axis=1)
    q_pad = jnp.concatenate([jnp.zeros((db, bq - ds, w_b), F32), q_new], axis=1).reshape(db * bq, w_b)
    o_b_s = _sb(q_pad, lambda b, p, i: (b, p), k_seq_s, v_seq_s,
                n_qtiles=1, out_rows=db * bq, out_block=lambda b, p, i: (b, p),
                bq=bq, bk=bk, n_null=null_s, q_tile0=tk_s // bq - 1)
    o_b = jnp.concatenate([o_b_p, o_b_s.reshape(db, bq, w_b)[:, bq - ds:].reshape(n_s, w_b)], axis=0)

    wr = jnp.concatenate([router_expert_w[0], router_group_w[0]], axis=1)
    wr = _pad_cols(wr, LANES)
    br = _pad_cols(jnp.concatenate([router_expert_b[0], router_group_b[0]])[None], LANES)
    tm4 = _pick_tile(rows_main, (256, 128))
    h1, comb = _mix(x_all, o_a, o_b, proj, gate_off, w_branch[0].astype(BF16), w_out[0].astype(BF16),
                    (ln_in_g[None], ln_in_b[None]), (ln_mix_g, ln_mix_b), wr, br,
                    rows=rows_main, tm=tm4, alpha=alpha, n_exp=n_exp, n_grp=n_grp)

    tm5 = _pick_tile(rows_main, (512, 256, 128))
    y = _moe(h1, comb, moe_w_gate[0].astype(BF16), moe_w_up[0].astype(BF16), moe_w_down[0].astype(BF16),
             ln_ffn_g, ln_ffn_b, tm=tm5, alpha=alpha)

    y_p = y[:n_p].reshape(nb, seq, d)
    y_s = y[n_p:].reshape(db, ds, d)
    heads = lambda a, n, t: a.reshape(1, n, t, h_b, HEAD_DIM)
    sb_k_p = heads(k_seq_p[:, null_p:], nb, n_meta + seq)
    sb_v_p = heads(v_seq_p[:, null_p:], nb, n_meta + seq)
    last_p = proj[:n_p].reshape(nb, seq, p_cols)[:, seq - 1, :pr_cols]
    last_s = proj[n_p:rows_main].reshape(db, ds, p_cols)[:, ds - 1, :pr_cols]
    shift_p = from_padded_rwkv_cols(last_p)[None, :, None, :]
    shift_s = from_padded_rwkv_cols(last_s)[None, :, None, :]
    return (y_p, y_s, sb_k_p, sb_v_p, from_hbd(hbd_p)[None], shift_p,
            heads(k_new, db, ds), heads(v_new, db, ds), from_hbd(hbd_s)[None], shift_s)
```

```python
import functools
import math

import jax
import jax.numpy as jnp
from jax import lax
from jax.experimental import pallas as pl
from jax.experimental.pallas import tpu as pltpu

F32 = jnp.float32
BF16 = jnp.bfloat16

HEAD_DIM = 64
LANES = 128
CHUNK = 64
TOP_K = 2
LN_EPS = 1e-5
GN_EPS = 64e-5
UNDERFLOW_LOG2 = 150.0
VMEM_LIMIT = 56 * 1024 * 1024


def _round_up(n, m):
    return -(-n // m) * m


def _pick_tile(n, candidates):
    for c in candidates:
        if n % c == 0:
            return c
    raise ValueError(f"no tile in {candidates} divides {n}")


def _params(sem, vmem=VMEM_LIMIT):
    return pltpu.CompilerParams(dimension_semantics=sem, vmem_limit_bytes=vmem)


def _dot(a, b):
    return jnp.dot(a.astype(BF16), b.astype(BF16), preferred_element_type=F32)


def _split2(x):
    hi = x.astype(BF16)
    lo = (x - hi.astype(F32)).astype(BF16)
    return hi, lo


def _split3(x):
    hi = x.astype(BF16)
    r1 = x - hi.astype(F32)
    mid = r1.astype(BF16)
    lo = (r1 - mid.astype(F32)).astype(BF16)
    return hi, mid, lo


def _dot3(a, b):
    ah, al = _split2(a)
    bh, bl = _split2(b)
    d = lambda x, y: jnp.dot(x, y, preferred_element_type=F32)
    return d(ah, bh) + (d(ah, bl) + d(al, bh))


def _dot3_nt(a, b):
    ah, al = _split2(a)
    bh, bl = _split2(b)
    d = lambda x, y: lax.dot_general(x, y, (((1,), (1,)), ((), ())), preferred_element_type=F32)
    return d(ah, bh) + (d(ah, bl) + d(al, bh))


def _dot_exact_rhs(a, b_bf16):
    hi, lo = _split2(a)
    d = lambda x: jnp.dot(x, b_bf16, preferred_element_type=F32)
    return d(hi) + d(lo)


def _dot_exact_lhs(a_bf16, b):
    hi, mid, lo = _split3(b)
    d = lambda x: jnp.dot(a_bf16, x, preferred_element_type=F32)
    return d(hi) + (d(mid) + d(lo))


def _layer_norm(x, g, b):
    mu = jnp.mean(x, axis=-1, keepdims=True)
    xc = x - mu
    var = jnp.mean(xc * xc, axis=-1, keepdims=True)
    return xc * lax.rsqrt(var + LN_EPS) * g + b


def _sigmoid(x):
    return 1.0 / (1.0 + jnp.exp(-x))


def _softplus(x):
    return jnp.maximum(x, 0.0) + jnp.log(1.0 + jnp.exp(-jnp.abs(x)))


def _iota(shape, dim):
    return lax.broadcasted_iota(jnp.int32, shape, dim)


def _inproj_kernel(x_ref, g_ref, b_ref, w_ref, o_ref, xn_ref):
    @pl.when(pl.program_id(1) == 0)
    def _():
        xn_ref[...] = _layer_norm(x_ref[...], g_ref[...], b_ref[...]).astype(BF16)

    o_ref[...] = jnp.dot(xn_ref[...], w_ref[...], preferred_element_type=F32)


def _inproj(x_all, ln_g, ln_b, w_pad, tm, tn):
    rows, d = x_all.shape
    cols = w_pad.shape[1]
    return pl.pallas_call(
        _inproj_kernel,
        grid=(rows // tm, cols // tn),
        in_specs=[
            pl.BlockSpec((tm, d), lambda i, j: (i, 0)),
            pl.BlockSpec((1, d), lambda i, j: (0, 0)),
            pl.BlockSpec((1, d), lambda i, j: (0, 0)),
            pl.BlockSpec((d, tn), lambda i, j: (0, j)),
        ],
        out_specs=pl.BlockSpec((tm, tn), lambda i, j: (i, j)),
        out_shape=jax.ShapeDtypeStruct((rows, cols), F32),
        scratch_shapes=[pltpu.VMEM((tm, d), BF16)],
        compiler_params=_params(("parallel", "arbitrary")),
        name="inproj",
    )(x_all, ln_g, ln_b, w_pad)


def _rwkv_kernel(p_ref, head_ref, prev_ref, h0_ref, mu_ref, w0_ref, a0_ref, kk_ref, ka_ref, rk_ref,
                 lng_ref, lnb_ref, wup_ref, aup_ref, gup_ref,
                 o_ref, hout_ref, carry_ref, hbd_ref, *, n_pairs, has_head, n_null, dw, da, dg):
    t = pl.program_id(1)
    nt = pl.num_programs(1)
    w_a = n_pairs * LANES
    c = CHUNK

    @pl.when(t == 0)
    def _():
        carry_ref[...] = prev_ref[0]
        hbd_ref[...] = h0_ref[0]

    if has_head:
        p = jnp.where(t == 0, head_ref[...], p_ref[...])
    else:
        p = p_ref[...]
    row = _iota((c, 1), 0)
    shifted = jnp.where(row == 0, carry_ref[...], pltpu.roll(p, 1, axis=0))
    carry_ref[...] = p[c - 1:c, :]
    xm = p + (shifted - p) * mu_ref[...]

    r = xm[:, 0:w_a]
    k = xm[:, w_a:2 * w_a]
    v = xm[:, 2 * w_a:3 * w_a]
    o = 3 * w_a
    xw = xm[:, o:o + dw]
    xa = xm[:, o + dw:o + dw + da]
    xg = xm[:, o + dw + da:o + dw + da + dg]

    w = -_softplus(-(w0_ref[...] + _dot3(jnp.tanh(xw), wup_ref[...]))) - 0.5
    lw = -jnp.exp(w)
    a = _sigmoid(a0_ref[...] + _dot3(xa, aup_ref[...]))
    g = _dot3(_sigmoid(xg), gup_ref[...])
    kk0 = k * kk_ref[...]
    kmod = k * (1.0 + (a - 1.0) * ka_ref[...])
    if has_head:
        valid = jnp.logical_or(row >= n_null, t > 0)
        lw = jnp.where(valid, lw, 0.0)
        kk0 = jnp.where(valid, kk0, 0.0)
        kmod = jnp.where(valid, kmod, 0.0)

    tri = (_iota((c, c), 1) <= _iota((c, c), 0)).astype(BF16)
    lcum = _dot_exact_lhs(tri, lw)
    lend = lcum[c - 1:c, :]
    e_pos = jnp.exp(lcum)
    e_prev = jnp.exp(lcum - lw)
    e_neg = jnp.exp(-lcum)
    e_end = jnp.exp(lend - lcum)

    lane = _iota((c, LANES), 1)
    rowc = _iota((c, LANES), 0)
    first = lane < HEAD_DIM
    col = jnp.where(first, lane, lane - HEAD_DIM)
    strict = col < rowc
    incl = col <= rowc
    eye = (col == rowc).astype(F32)
    rr = _iota((LANES, LANES), 0)
    cc = _iota((LANES, LANES), 1)
    bd_mask = (rr < HEAD_DIM) == (cc < HEAD_DIM)
    ones_bd = bd_mask.astype(BF16)
    avg_bd = (bd_mask.astype(F32) * (1.0 / HEAD_DIM)).astype(BF16)
    ones_cl = jnp.ones((c, LANES), BF16)

    def bd(x):
        return jnp.concatenate([jnp.where(first, x, 0.0), jnp.where(first, 0.0, x)], axis=0)

    def pack(x1, x2, left):
        if left:
            return jnp.where(first, x1, pltpu.roll(x2, HEAD_DIM, axis=1))
        return jnp.where(first, pltpu.roll(x1, HEAD_DIM, axis=1), x2)

    prs = range(n_pairs)
    sls = [slice(pr * LANES, (pr + 1) * LANES) for pr in prs]
    hs = [hbd_ref[pr] for pr in prs]
    kk0p = [kk0[:, sl] for sl in sls]
    ss = [_dot_exact_rhs(x * x, ones_bd) for x in kk0p]
    kkn = [x / jnp.maximum(jnp.sqrt(s_), 1e-12) for x, s_ in zip(kk0p, ss)]
    bb = [kkn[pr] * a[:, sls[pr]] for pr in prs]
    a_t = [-kkn[pr] * e_prev[:, sls[pr]] for pr in prs]
    b_t = [bb[pr] * e_neg[:, sls[pr]] for pr in prs]
    k_t = [kmod[:, sls[pr]] * e_neg[:, sls[pr]] for pr in prs]
    r_t = [r[:, sls[pr]] * e_pos[:, sls[pr]] for pr in prs]
    b_h = [bb[pr] * e_end[:, sls[pr]] for pr in prs]
    k_h = [kmod[:, sls[pr]] * e_end[:, sls[pr]] for pr in prs]
    vp = [v[:, sl] for sl in sls]
    vbd = [bd(x) for x in vp]

    lhs = [jnp.concatenate([jnp.where(first, a_t[pr], 0.0), jnp.where(first, 0.0, a_t[pr]),
                            jnp.where(first, r_t[pr], 0.0), jnp.where(first, 0.0, r_t[pr])], axis=0) for pr in prs]
    rhs = [jnp.concatenate([b_t[pr], k_t[pr]], axis=0) for pr in prs]
    sc = [_dot3_nt(lhs[pr], rhs[pr]) for pr in prs]
    m_ab = [jnp.where(strict, pack(s_[0:c], s_[c:2 * c], True), 0.0) for s_ in sc]
    m_ak = [jnp.where(strict, pack(s_[0:c], s_[c:2 * c], False), 0.0) for s_ in sc]
    m_rb = [jnp.where(incl, pack(s_[2 * c:3 * c], s_[3 * c:4 * c], True), 0.0) for s_ in sc]
    m_rk = [jnp.where(incl, pack(s_[2 * c:3 * c], s_[3 * c:4 * c], False), 0.0) for s_ in sc]

    same = lambda m: (col >> (m.bit_length() - 1)) == (rowc >> (m.bit_length() - 1))
    tinv = [eye + jnp.where(same(2), x, 0.0) for x in m_ab]
    m = 4
    while m <= c:
        off_mask = jnp.logical_and(same(m), jnp.logical_not(same(m // 2)))
        a_off = [jnp.where(off_mask, x, 0.0) for x in m_ab]
        left = [_dot(tinv[pr], bd(a_off[pr])) for pr in prs]
        tinv = [tinv[pr] + _dot(left[pr], bd(tinv[pr])) for pr in prs]
        m *= 2

    ah = [_dot3(a_t[pr], hs[pr]) for pr in prs]
    akv = [_dot(m_ak[pr], vbd[pr]) for pr in prs]
    rh = [_dot3(r_t[pr], hs[pr]) for pr in prs]
    rkv = [_dot(m_rk[pr], vbd[pr]) for pr in prs]
    u = [_dot3(tinv[pr], bd(ah[pr] + akv[pr])) for pr in prs]
    y = [rh[pr] + rkv[pr] + _dot(m_rb[pr], bd(u[pr])) for pr in prs]

    lend_col = [_dot_exact_rhs(lw[:, sl].T, ones_cl) for sl in sls]
    upd = [_dot(jnp.concatenate([b_h[pr], k_h[pr]], axis=0).T, jnp.concatenate([u[pr], vp[pr]], axis=0))
           for pr in prs]
    h_new = [jnp.where(bd_mask, hs[pr] * jnp.exp(lend_col[pr]) + upd[pr], 0.0) for pr in prs]

    mean = [_dot_exact_rhs(x, avg_bd) for x in y]
    yc = [y[pr] - mean[pr] for pr in prs]
    var = [_dot_exact_rhs(x * x, avg_bd) for x in yc]
    bonus = [_dot_exact_rhs(r[:, sls[pr]] * kmod[:, sls[pr]] * rk_ref[:, sls[pr]], ones_bd) * vp[pr] for pr in prs]
    for pr in prs:
        sl = sls[pr]
        yn = yc[pr] * lax.rsqrt(var[pr] + GN_EPS) * lng_ref[:, sl] + lnb_ref[:, sl]
        o_ref[:, sl] = ((yn + bonus[pr]) * g[:, sl]).astype(o_ref.dtype)
    for pr in prs:
        hbd_ref[pr] = h_new[pr]

    @pl.when(t == nt - 1)
    def _():
        hout_ref[0] = hbd_ref[...]


def _rwkv(proj, head, prev, h0, rw, *, n_seq, n_tiles, row_block, out_block, out_rows, has_head, n_null, pr_cols,
          widths):
    n_pairs = h0.shape[1]
    w_a = n_pairs * LANES
    dw, da, dg = widths
    vec = lambda n: pl.BlockSpec((1, n), lambda s, t: (0, 0))
    mat = lambda a: pl.BlockSpec(a.shape, lambda s, t: (0, 0))
    kern = functools.partial(_rwkv_kernel, n_pairs=n_pairs, has_head=has_head, n_null=n_null,
                             dw=dw, da=da, dg=dg)
    return pl.pallas_call(
        kern,
        grid=(n_seq, n_tiles),
        in_specs=[
            pl.BlockSpec((CHUNK, pr_cols), lambda s, t: (row_block(s, t), 0)),
            pl.BlockSpec((CHUNK, pr_cols), lambda s, t: (0, 0)),
            pl.BlockSpec((1, 1, pr_cols), lambda s, t: (s, 0, 0)),
            pl.BlockSpec((1, n_pairs, LANES, LANES), lambda s, t: (s, 0, 0, 0)),
            vec(pr_cols), vec(w_a), vec(w_a), vec(w_a), vec(w_a), vec(w_a), vec(w_a), vec(w_a),
            mat(rw["w_up"]), mat(rw["a_up"]), mat(rw["g_up"]),
        ],
        out_specs=[
            pl.BlockSpec((CHUNK, w_a), lambda s, t: (out_block(s, t), 0)),
            pl.BlockSpec((1, n_pairs, LANES, LANES), lambda s, t: (s, 0, 0, 0)),
        ],
        out_shape=[
            jax.ShapeDtypeStruct((out_rows, w_a), BF16),
            jax.ShapeDtypeStruct((n_seq, n_pairs, LANES, LANES), F32),
        ],
        scratch_shapes=[pltpu.VMEM((1, pr_cols), F32), pltpu.VMEM((n_pairs, LANES, LANES), F32)],
        compiler_params=_params(("parallel", "arbitrary")),
        name="rwkv_head" if has_head else "rwkv",
    )(proj, head, prev, h0, rw["mu"], rw["w0"], rw["a0"], rw["k_k"], rw["k_a"], rw["r_k"],
      rw["lnx_g"], rw["lnx_b"], rw["w_up"], rw["a_up"], rw["g_up"])


def _sb_kernel(q_ref, km_ref, vm_ref, ke_ref, ve_ref, o_ref, kt_ref, vb_ref, qm_ref, acc_ref, c_ref, *,
               bq, bk, rc, n_main, extra_valid, extra_newest, scale):
    qi = pl.program_id(2)
    sub = bk // LANES

    @pl.when(qi == 0)
    def _():
        def fill(j, carry):
            for s in range(sub):
                blk = km_ref[pl.ds(pl.multiple_of(j * bk + s * LANES, LANES), LANES), :]
                kt_ref[j, :, s * LANES:(s + 1) * LANES] = blk.T.astype(BF16)
            return carry

        lax.fori_loop(0, n_main, fill, 0)
        for s in range(sub):
            kt_ref[n_main, :, s * LANES:(s + 1) * LANES] = ke_ref[s * LANES:(s + 1) * LANES, :].T.astype(BF16)
        vb_ref[0:n_main * bk, :] = vm_ref[...].astype(BF16)
        vb_ref[n_main * bk:(n_main + 1) * bk, :] = ve_ref[...].astype(BF16)

    lane_q = _iota((bq, LANES), 1)
    q = q_ref[...] * (scale * math.log2(math.e))
    qm_ref[0] = jnp.where(lane_q < HEAD_DIM, q, 0.0).astype(BF16)
    qm_ref[1] = jnp.where(lane_q < HEAD_DIM, 0.0, q).astype(BF16)
    later = (_iota((bk, bk), 0) > _iota((bk, bk), 1)).astype(BF16)
    col = _iota((rc, bk), 1)
    row = _iota((rc, bk), 0)
    acc_ref[...] = jnp.zeros_like(acc_ref)
    c_ref[...] = jnp.zeros_like(c_ref)

    def block(j, mask_fn):
        kt = kt_ref[j]
        vblk = vb_ref[pl.ds(pl.multiple_of(j * bk, bk), bk), :]
        units = []
        for r0 in range(0, bq, rc):
            mask = None
            if mask_fn is not None:
                mask = mask_fn(r0)
                if mask is None:
                    continue
            units += [dict(r0=r0, hh=hh, mask=mask) for hh in range(2)]

        def logits(u):
            z = jnp.dot(qm_ref[u["hh"], u["r0"]:u["r0"] + rc, :], kt, preferred_element_type=F32)
            neg_abs = pltpu.bitcast(pltpu.bitcast(z, jnp.uint32) | jnp.uint32(0x80000000), F32)
            nlr = jnp.maximum(z, 0.0) + jnp.log2(1.0 + jnp.exp2(neg_abs))
            if u["mask"] is not None:
                nlr = jnp.where(u["mask"], nlr, 0.0)
            u.update(z=z, nlr=nlr, nlr16=nlr.astype(BF16), rowsum=jnp.sum(nlr, axis=1, keepdims=True))

        def newer(u):
            u["between"] = jnp.dot(u["nlr16"], later, preferred_element_type=F32)

        def weigh(u):
            rows = slice(u["r0"], u["r0"] + rc)
            cb = c_ref[u["hh"], rows, :]
            wgt = jnp.exp2(u["z"] - u["nlr"] - u["between"] - jnp.concatenate([cb] * sub, axis=1))
            if u["mask"] is not None:
                wgt = jnp.where(u["mask"], wgt, 0.0)
            acc_ref[u["hh"], rows, :] += jnp.dot(wgt.astype(BF16), vblk, preferred_element_type=F32)
            c_ref[u["hh"], rows, :] = cb + u["rowsum"]

        stages = (logits, newer, weigh)
        for step in range(len(units) + len(stages) - 1):
            for s_idx, stage in enumerate(stages):
                if 0 <= step - s_idx < len(units):
                    stage(units[step - s_idx])

    def passed():
        return jnp.min(jnp.minimum(c_ref[0], c_ref[1]))

    def sweep(j_first):
        def cond(carry):
            n, c_min = carry
            return jnp.logical_and(n < j_first, c_min < UNDERFLOW_LOG2)

        def body(carry):
            block(j_first - 1 - carry[0], None)
            return carry[0] + 1, passed()

        return lax.while_loop(cond, body, (jnp.int32(0), passed()))[1]

    if extra_newest:
        block(n_main, lambda r0: jnp.logical_and(col < row + r0, col < extra_valid))
        sweep(n_main)
    else:
        j_lo = (qi * bq) // bk
        for d in range(bq // bk - 1, -1, -1):
            block(j_lo + d, lambda r0, d=d: None if d * bk >= r0 + rc - 1 else d * bk + col < row + r0)
        c_min = sweep(j_lo)

        @pl.when(c_min < UNDERFLOW_LOG2)
        def _():
            block(n_main, lambda r0: col < extra_valid)
    o_ref[...] = jnp.where(lane_q < HEAD_DIM, acc_ref[0], acc_ref[1]).astype(o_ref.dtype)


def _sb(q_arr, q_block, k_main, k_block, v_main, v_block, main_rows, k_extra, v_extra, extra_block, *, grid,
        out_rows, out_block, bq, bk, extra_valid, extra_newest, rc=256):
    w_b = k_extra.shape[1]
    n_main = main_rows // bk
    assert extra_newest or bq % bk == 0
    kern = functools.partial(_sb_kernel, bq=bq, bk=bk, rc=min(rc, bq), n_main=n_main, extra_valid=extra_valid,
                             extra_newest=extra_newest, scale=HEAD_DIM ** -0.5)
    return pl.pallas_call(
        kern,
        grid=grid,
        in_specs=[
            pl.BlockSpec((bq, LANES), q_block),
            pl.BlockSpec((main_rows, LANES), k_block),
            pl.BlockSpec((main_rows, LANES), v_block),
            pl.BlockSpec((bk, LANES), extra_block),
            pl.BlockSpec((bk, LANES), extra_block),
        ],
        out_specs=pl.BlockSpec((bq, LANES), out_block),
        out_shape=jax.ShapeDtypeStruct((out_rows, w_b), BF16),
        scratch_shapes=[
            pltpu.VMEM((n_main + 1, LANES, bk), BF16),
            pltpu.VMEM(((n_main + 1) * bk, LANES), BF16),
            pltpu.VMEM((2, bq, LANES), BF16),
            pltpu.VMEM((2, bq, LANES), F32),
            pltpu.VMEM((2, bq, LANES), F32),
        ],
        compiler_params=_params(("parallel", "parallel", "arbitrary")),
        name="stick_breaking_newest" if extra_newest else "stick_breaking",
    )(q_arr, k_main, v_main, k_extra, v_extra)


def _mix_kernel(*refs, n_gc, gw, w_a, alpha, n_exp, n_grp):
    x_ref, oa_ref, ob_ref = refs[0:3]
    ga_refs = refs[3:3 + n_gc]
    gb_refs = refs[3 + n_gc:3 + 2 * n_gc]
    (wb_ref, wo_ref, lig_ref, lib_ref, lmg_ref, lmb_ref, wr_ref, br_ref,
     h_ref, comb_ref, mg_ref) = refs[3 + 2 * n_gc:]
    oa = oa_ref[...]
    ob = ob_ref[...]
    for cidx in range(n_gc):
        cs = slice(cidx * gw, (cidx + 1) * gw)
        pa = jnp.dot(oa, wb_ref[0:w_a, cs], preferred_element_type=F32)
        pb = jnp.dot(ob, wb_ref[w_a:, cs], preferred_element_type=F32)
        mg_ref[:, cs] = (_sigmoid(ga_refs[cidx][...]) * pa + _sigmoid(gb_refs[cidx][...]) * pb).astype(BF16)
    mix = jnp.dot(mg_ref[...], wo_ref[...], preferred_element_type=F32)
    h = _layer_norm(x_ref[...], lig_ref[...], lib_ref[...])
    h1 = _layer_norm(alpha * h + mix, lmg_ref[...], lmb_ref[...])
    h_ref[...] = h1

    lg = _dot3(h1, wr_ref[...]) + br_ref[...]
    lane = _iota(lg.shape, 1).astype(F32)
    neg, big = -1e30, 1e9
    epg = n_exp // n_grp
    is_g = jnp.logical_and(lane >= n_exp, lane < n_exp + n_grp)
    gmax = jnp.max(jnp.where(is_g, lg, neg), axis=1, keepdims=True)
    gsum = jnp.sum(jnp.where(is_g, jnp.exp(jnp.where(is_g, lg, neg) - gmax), 0.0), axis=1, keepdims=True)
    p_g = 1.0 / gsum
    g_idx = jnp.min(jnp.where(jnp.logical_and(is_g, lg == gmax), lane - n_exp, big), axis=1, keepdims=True)
    in_grp = jnp.logical_and(lane >= g_idx * epg, lane < (g_idx + 1.0) * epg)
    el = jnp.where(in_grp, lg, neg)
    m1 = jnp.max(el, axis=1, keepdims=True)
    i1 = jnp.min(jnp.where(jnp.logical_and(in_grp, lg == m1), lane, big), axis=1, keepdims=True)
    el2 = jnp.where(lane == i1, neg, el)
    m2 = jnp.max(el2, axis=1, keepdims=True)
    i2 = jnp.min(jnp.where(jnp.logical_and(el2 == m2, in_grp), lane, big), axis=1, keepdims=True)
    e2 = jnp.exp(m2 - m1)
    w1 = p_g / (1.0 + e2)
    w2 = p_g * e2 / (1.0 + e2)
    comb_ref[...] = jnp.where(lane == i1, w1, 0.0) + jnp.where(lane == i2, w2, 0.0)


def _mix(x_all, o_a, o_b, proj, gate_off, wb, wo, ln_in, ln_mix, wr, br, *, rows, tm, alpha, n_exp, n_grp):
    d = x_all.shape[1]
    w_a = o_a.shape[1]
    gw = math.gcd(gate_off, d)
    n_gc = d // gw
    gspec = lambda blk: pl.BlockSpec((tm, gw), lambda i, blk=blk: (i, blk))
    const = lambda a: pl.BlockSpec(a.shape, lambda i: (0,) * a.ndim, pipeline_mode=pl.Buffered(1))
    kern = functools.partial(_mix_kernel, n_gc=n_gc, gw=gw, w_a=w_a, alpha=alpha, n_exp=n_exp, n_grp=n_grp)
    g0 = gate_off // gw
    args = [x_all, o_a, o_b] + [proj] * (2 * n_gc) + [wb, wo, ln_in[0], ln_in[1], ln_mix[0], ln_mix[1], wr, br]
    in_specs = ([pl.BlockSpec((tm, d), lambda i: (i, 0)),
                 pl.BlockSpec((tm, w_a), lambda i: (i, 0)),
                 pl.BlockSpec((tm, o_b.shape[1]), lambda i: (i, 0))]
                + [gspec(g0 + cidx) for cidx in range(n_gc)]
                + [gspec(g0 + n_gc + cidx) for cidx in range(n_gc)]
                + [const(a) for a in args[3 + 2 * n_gc:]])
    return pl.pallas_call(
        kern,
        grid=(rows // tm,),
        in_specs=in_specs,
        out_specs=[pl.BlockSpec((tm, d), lambda i: (i, 0)), pl.BlockSpec((tm, LANES), lambda i: (i, 0))],
        out_shape=[jax.ShapeDtypeStruct((rows, d), F32), jax.ShapeDtypeStruct((rows, LANES), F32)],
        scratch_shapes=[pltpu.VMEM((tm, d), BF16)],
        compiler_params=_params(("parallel",)),
        name="mix_out_router",
    )(*args)


def _moe_kernel(x_ref, c_ref, wg_ref, wu_ref, wd_ref, g_ref, b_ref, o_ref, xb_ref, acc_ref, *, alpha):
    e = pl.program_id(1)

    @pl.when(e == 0)
    def _():
        xb_ref[...] = x_ref[...].astype(BF16)
        acc_ref[...] = jnp.zeros_like(acc_ref)

    xb = xb_ref[...]
    hg = jnp.dot(xb, wg_ref[0], preferred_element_type=F32)
    hu = jnp.dot(xb, wu_ref[0], preferred_element_type=F32)
    cmb = c_ref[...]
    ce = jnp.sum(jnp.where(_iota(cmb.shape, 1) == e, cmb, 0.0), axis=1, keepdims=True)
    h = hg * _sigmoid(hg) * hu * ce
    acc_ref[...] += jnp.dot(h.astype(BF16), wd_ref[0], preferred_element_type=F32)

    @pl.when(e == pl.num_programs(1) - 1)
    def _():
        o_ref[...] = _layer_norm(alpha * x_ref[...] + acc_ref[...], g_ref[...], b_ref[...])


def _moe(h1, comb, wg, wu, wd, ln_g, ln_b, *, tm, alpha):
    rows, d = h1.shape
    n_exp, _, d_e = wg.shape
    return pl.pallas_call(
        functools.partial(_moe_kernel, alpha=alpha),
        grid=(rows // tm, n_exp),
        in_specs=[
            pl.BlockSpec((tm, d), lambda i, e: (i, 0)),
            pl.BlockSpec((tm, LANES), lambda i, e: (i, 0)),
            pl.BlockSpec((1, d, d_e), lambda i, e: (e, 0, 0)),
            pl.BlockSpec((1, d, d_e), lambda i, e: (e, 0, 0)),
            pl.BlockSpec((1, d_e, d), lambda i, e: (e, 0, 0)),
            pl.BlockSpec((1, d), lambda i, e: (0, 0)),
            pl.BlockSpec((1, d), lambda i, e: (0, 0)),
        ],
        out_specs=pl.BlockSpec((tm, d), lambda i, e: (i, 0)),
        out_shape=jax.ShapeDtypeStruct((rows, d), F32),
        scratch_shapes=[pltpu.VMEM((tm, d), BF16), pltpu.VMEM((tm, d), F32)],
        compiler_params=_params(("parallel", "arbitrary")),
        name="moe",
    )(h1, comb, wg, wu, wd, ln_g, ln_b)


def _pad_cols(a, n):
    return jnp.pad(a, [(0, 0)] * (a.ndim - 1) + [(0, n - a.shape[-1])])


def kernel(x_prompt, x_sample, cache_sb_k, cache_sb_v, state_rwkv, state_rwkv_shift, meta_tokens, ln_in_g, ln_in_b, w_in, rwkv_mu, rwkv_w0, rwkv_w_up, rwkv_a0, rwkv_a_up, rwkv_g_up, rwkv_k_k, rwkv_k_a, rwkv_r_k, rwkv_lnx_g, rwkv_lnx_b, w_branch, w_out, ln_mix_g, ln_mix_b, router_group_w, router_group_b, router_expert_w, router_expert_b, moe_w_gate, moe_w_up, moe_w_down, ln_ffn_g, ln_ffn_b):
    depth = w_in.shape[0]
    assert depth == 1, "single-layer step only"
    nb, seq, d = x_prompt.shape
    db, ds = x_sample.shape[:2]
    n_meta = meta_tokens.shape[0]
    past = cache_sb_k.shape[2]
    w_a = rwkv_w0.shape[1]
    r_d, r_a, r_g = rwkv_w_up.shape[1], rwkv_a_up.shape[1], rwkv_g_up.shape[1]
    w_shift = 3 * w_a + r_d + r_a + r_g
    w_b = (w_in.shape[2] - w_shift - 2 * d) // 3
    h_a, h_b = w_a // HEAD_DIM, w_b // HEAD_DIM
    n_grp = router_group_w.shape[2]
    n_exp = router_expert_w.shape[2]
    alpha = (2 * depth) ** 0.25
    assert w_a % LANES == 0 and w_b % LANES == 0 and seq % 256 == 0 and ds == CHUNK and n_meta <= CHUNK
    assert n_exp + n_grp <= LANES

    dw, da, dg = (_round_up(n, LANES) for n in (r_d, r_a, r_g))
    pr_cols = 3 * w_a + dw + da + dg
    wi = w_in[0]
    s0 = 3 * w_a
    segs = [wi[:, :s0], _pad_cols(wi[:, s0:s0 + r_d], dw), _pad_cols(wi[:, s0 + r_d:s0 + r_d + r_a], da),
            _pad_cols(wi[:, s0 + r_d + r_a:w_shift], dg), wi[:, w_shift:]]
    w_pad = jnp.concatenate(segs, axis=1).astype(BF16)
    p_cols = w_pad.shape[1]
    q_off, k_off, v_off, gate_off = pr_cols, pr_cols + w_b, pr_cols + 2 * w_b, pr_cols + 3 * w_b

    def to_padded_rwkv_cols(a):
        return jnp.concatenate([a[..., :s0], _pad_cols(a[..., s0:s0 + r_d], dw),
                                _pad_cols(a[..., s0 + r_d:s0 + r_d + r_a], da),
                                _pad_cols(a[..., s0 + r_d + r_a:], dg)], axis=-1)

    def from_padded_rwkv_cols(a):
        return jnp.concatenate([a[..., :s0], a[..., s0:s0 + r_d], a[..., s0 + dw:s0 + dw + r_a],
                                a[..., s0 + dw + da:s0 + dw + da + r_g]], axis=-1)

    n_p, n_s = nb * seq, db * ds
    rows_main = n_p + n_s
    tm1 = _pick_tile(rows_main, (512, 256, 128))
    rows_all = _round_up(rows_main + n_meta, tm1)
    x_all = jnp.concatenate([x_prompt.reshape(n_p, d), x_sample.reshape(n_s, d), meta_tokens.astype(F32),
                             jnp.zeros((rows_all - rows_main - n_meta, d), F32)], axis=0)
    tn1 = _pick_tile(p_cols, (1792, 512, 256, 128))
    proj = _inproj(x_all, ln_in_g[None], ln_in_b[None], w_pad, tm1, tn1)
    proj_meta = proj[rows_main:rows_main + n_meta]

    row2 = lambda a: a.reshape(1, -1)
    rw = dict(mu=to_padded_rwkv_cols(rwkv_mu[0])[None], w0=row2(rwkv_w0[0]), a0=row2(rwkv_a0[0]),
              k_k=row2(rwkv_k_k[0]), k_a=row2(rwkv_k_a[0]), r_k=row2(rwkv_r_k[0]),
              lnx_g=row2(rwkv_lnx_g[0]), lnx_b=row2(rwkv_lnx_b[0]),
              w_up=jnp.pad(rwkv_w_up[0], ((0, dw - r_d), (0, 0))),
              a_up=jnp.pad(rwkv_a_up[0], ((0, da - r_a), (0, 0))),
              g_up=jnp.pad(rwkv_g_up[0], ((0, dg - r_g), (0, 0))))
    n_pairs_a = w_a // LANES
    n_null = CHUNK - n_meta
    head = jnp.concatenate([jnp.zeros((n_null, pr_cols), F32), proj_meta[:, :pr_cols]], axis=0)
    tiles_p = seq // CHUNK
    o_a_p, hbd_p = _rwkv(proj, head, jnp.zeros((nb, 1, pr_cols), F32),
                         jnp.zeros((nb, n_pairs_a, LANES, LANES), F32), rw,
                         n_seq=nb, n_tiles=tiles_p + 1,
                         row_block=lambda s, t: s * tiles_p + jnp.maximum(t - 1, 0),
                         out_block=lambda s, t: s * tiles_p + jnp.maximum(t - 1, 0), out_rows=n_p, has_head=True, n_null=n_null, pr_cols=pr_cols, widths=(dw, da, dg))

    def to_hbd(state):
        n = state.shape[0]
        hkv = jnp.swapaxes(state, -1, -2).reshape(n, n_pairs_a, 2, HEAD_DIM, HEAD_DIM)
        z = jnp.zeros_like(hkv[:, :, 0])
        top = jnp.concatenate([hkv[:, :, 0], z], axis=-1)
        bot = jnp.concatenate([z, hkv[:, :, 1]], axis=-1)
        return jnp.concatenate([top, bot], axis=-2)

    def from_hbd(hbd):
        h0 = hbd[:, :, :HEAD_DIM, :HEAD_DIM]
        h1 = hbd[:, :, HEAD_DIM:, HEAD_DIM:]
        hkv = jnp.stack([h0, h1], axis=2).reshape(hbd.shape[0], h_a, HEAD_DIM, HEAD_DIM)
        return jnp.swapaxes(hkv, -1, -2)

    base_s = n_p // CHUNK
    o_a_s, hbd_s = _rwkv(proj, head, to_padded_rwkv_cols(state_rwkv_shift[0]), to_hbd(state_rwkv[0]), rw,
                         n_seq=db, n_tiles=ds // CHUNK,
                         row_block=lambda s, t: base_s + s * (ds // CHUNK) + t,
                         out_block=lambda s, t: s * (ds // CHUNK) + t, out_rows=n_s, has_head=False, n_null=0, pr_cols=pr_cols, widths=(dw, da, dg))
    o_a = jnp.concatenate([o_a_p, o_a_s], axis=0)

    bk = 256
    bq_p = _pick_tile(seq, (512, 256))
    assert past % bk == 0 and n_meta <= bk and ds <= bk
    n_pairs_b = w_b // LANES
    qc, kc, vc = q_off // LANES, k_off // LANES, v_off // LANES
    k_meta, v_meta = proj_meta[:, k_off:k_off + w_b], proj_meta[:, v_off:v_off + w_b]
    pad_rows = lambda a: jnp.pad(a, ((0, bk - a.shape[0]), (0, 0)))
    qt_p = seq // bq_p
    o_b_p = _sb(proj, lambda b, p, i: (b * qt_p + i, qc + p),
                proj, lambda b, p, i: (b, kc + p), proj, lambda b, p, i: (b, vc + p), seq,
                pad_rows(k_meta), pad_rows(v_meta), lambda b, p, i: (0, p),
                grid=(nb, n_pairs_b, qt_p), out_rows=n_p, out_block=lambda b, p, i: (b * qt_p + i, p),
                bq=bq_p, bk=bk, extra_valid=n_meta, extra_newest=False)

    new = lambda off: proj[n_p:rows_main, off:off + w_b].reshape(db, ds, w_b)
    k_new, v_new = new(k_off), new(v_off)
    pad_seq = lambda a: jnp.pad(a, ((0, 0), (0, bk - ds), (0, 0))).reshape(db * bk, w_b)
    o_b_s = _sb(proj, lambda b, p, i: (n_p // ds + b, qc + p),
                cache_sb_k[0].reshape(db * past, w_b), lambda b, p, i: (b, p),
                cache_sb_v[0].reshape(db * past, w_b), lambda b, p, i: (b, p), past,
                pad_seq(k_new), pad_seq(v_new), lambda b, p, i: (b, p),
                grid=(db, n_pairs_b, 1), out_rows=n_s, out_block=lambda b, p, i: (b, p),
                bq=ds, bk=bk, extra_valid=ds, extra_newest=True)
    o_b = jnp.concatenate([o_b_p, o_b_s], axis=0)

    wr = jnp.concatenate([router_expert_w[0], router_group_w[0]], axis=1)
    wr = _pad_cols(wr, LANES)
    br = _pad_cols(jnp.concatenate([router_expert_b[0], router_group_b[0]])[None], LANES)
    tm4 = _pick_tile(rows_main, (256, 128))
    h1, comb = _mix(x_all, o_a, o_b, proj, gate_off, w_branch[0].astype(BF16), w_out[0].astype(BF16),
                    (ln_in_g[None], ln_in_b[None]), (ln_mix_g, ln_mix_b), wr, br,
                    rows=rows_main, tm=tm4, alpha=alpha, n_exp=n_exp, n_grp=n_grp)

    tm5 = _pick_tile(rows_main, (512, 256, 128))
    y = _moe(h1, comb, moe_w_gate[0].astype(BF16), moe_w_up[0].astype(BF16), moe_w_down[0].astype(BF16),
             ln_ffn_g, ln_ffn_b, tm=tm5, alpha=alpha)

    y_p = y[:n_p].reshape(nb, seq, d)
    y_s = y[n_p:].reshape(db, ds, d)
    heads = lambda a, n, t: a.reshape(1, n, t, h_b, HEAD_DIM)
    with_meta = lambda m, off: jnp.concatenate(
        [jnp.broadcast_to(m[None], (nb, n_meta, w_b)), proj[:n_p, off:off + w_b].reshape(nb, seq, w_b)], axis=1)
    sb_k_p = heads(with_meta(k_meta, k_off), nb, n_meta + seq)
    sb_v_p = heads(with_meta(v_meta, v_off), nb, n_meta + seq)
    last_p = proj[seq - 1:n_p:seq, :pr_cols]
    last_s = proj[n_p + ds - 1:rows_main:ds, :pr_cols]
    shift_p = from_padded_rwkv_cols(last_p)[None, :, None, :]
    shift_s = from_padded_rwkv_cols(last_s)[None, :, None, :]
    return (y_p, y_s, sb_k_p, sb_v_p, from_hbd(hbd_p)[None], shift_p,
            heads(k_new, db, ds), heads(v_new, db, ds), from_hbd(hbd_s)[None], shift_s)
```

```python
import functools
import math

import jax
import jax.numpy as jnp
from jax import lax
from jax.experimental import pallas as pl
from jax.experimental.pallas import tpu as pltpu

F32 = jnp.float32
BF16 = jnp.bfloat16

HEAD_DIM = 64
LANES = 128
CHUNK = 64
TOP_K = 2
LN_EPS = 1e-5
GN_EPS = 64e-5
UNDERFLOW_LOG2 = 150.0
VMEM_LIMIT = 56 * 1024 * 1024


def _round_up(n, m):
    return -(-n // m) * m


def _pick_tile(n, candidates):
    for c in candidates:
        if n % c == 0:
            return c
    raise ValueError(f"no tile in {candidates} divides {n}")


def _params(sem, vmem=VMEM_LIMIT):
    return pltpu.CompilerParams(dimension_semantics=sem, vmem_limit_bytes=vmem)


def _dot(a, b):
    return jnp.dot(a.astype(BF16), b.astype(BF16), preferred_element_type=F32)


def _split2(x):
    hi = x.astype(BF16)
    lo = (x - hi.astype(F32)).astype(BF16)
    return hi, lo


def _split3(x):
    hi = x.astype(BF16)
    r1 = x - hi.astype(F32)
    mid = r1.astype(BF16)
    lo = (r1 - mid.astype(F32)).astype(BF16)
    return hi, mid, lo


def _dot3(a, b):
    ah, al = _split2(a)
    bh, bl = _split2(b)
    d = lambda x, y: jnp.dot(x, y, preferred_element_type=F32)
    return d(ah, bh) + (d(ah, bl) + d(al, bh))


def _dot3_nt(a, b):
    ah, al = _split2(a)
    bh, bl = _split2(b)
    d = lambda x, y: lax.dot_general(x, y, (((1,), (1,)), ((), ())), preferred_element_type=F32)
    return d(ah, bh) + (d(ah, bl) + d(al, bh))


def _dot_exact_rhs(a, b_bf16):
    hi, lo = _split2(a)
    d = lambda x: jnp.dot(x, b_bf16, preferred_element_type=F32)
    return d(hi) + d(lo)


def _dot_exact_lhs(a_bf16, b):
    hi, mid, lo = _split3(b)
    d = lambda x: jnp.dot(a_bf16, x, preferred_element_type=F32)
    return d(hi) + (d(mid) + d(lo))


def _layer_norm(x, g, b):
    mu = jnp.mean(x, axis=-1, keepdims=True)
    xc = x - mu
    var = jnp.mean(xc * xc, axis=-1, keepdims=True)
    return xc * lax.rsqrt(var + LN_EPS) * g + b


def _sigmoid(x):
    return 1.0 / (1.0 + jnp.exp(-x))


def _softplus(x):
    return jnp.maximum(x, 0.0) + jnp.log(1.0 + jnp.exp(-jnp.abs(x)))


def _iota(shape, dim):
    return lax.broadcasted_iota(jnp.int32, shape, dim)


def _inproj_kernel(x_ref, g_ref, b_ref, w_ref, o_ref, xn_ref):
    @pl.when(pl.program_id(1) == 0)
    def _():
        xn_ref[...] = _layer_norm(x_ref[...], g_ref[...], b_ref[...]).astype(BF16)

    o_ref[...] = jnp.dot(xn_ref[...], w_ref[...], preferred_element_type=F32)


def _inproj(x_all, ln_g, ln_b, w_pad, tm, tn):
    rows, d = x_all.shape
    cols = w_pad.shape[1]
    return pl.pallas_call(
        _inproj_kernel,
        grid=(rows // tm, cols // tn),
        in_specs=[
            pl.BlockSpec((tm, d), lambda i, j: (i, 0)),
            pl.BlockSpec((1, d), lambda i, j: (0, 0)),
            pl.BlockSpec((1, d), lambda i, j: (0, 0)),
            pl.BlockSpec((d, tn), lambda i, j: (0, j)),
        ],
        out_specs=pl.BlockSpec((tm, tn), lambda i, j: (i, j)),
        out_shape=jax.ShapeDtypeStruct((rows, cols), F32),
        scratch_shapes=[pltpu.VMEM((tm, d), BF16)],
        compiler_params=_params(("parallel", "arbitrary")),
        name="inproj",
    )(x_all, ln_g, ln_b, w_pad)


def _rwkv_kernel(p_ref, head_ref, prev_ref, h0_ref, mu_ref, w0_ref, a0_ref, kk_ref, ka_ref, rk_ref,
                 lng_ref, lnb_ref, wup_ref, aup_ref, gup_ref,
                 o_ref, hout_ref, carry_ref, hbd_ref, *, n_pairs, has_head, n_null, dw, da, dg):
    t = pl.program_id(1)
    nt = pl.num_programs(1)
    w_a = n_pairs * LANES
    c = CHUNK

    @pl.when(t == 0)
    def _():
        carry_ref[...] = prev_ref[0]
        hbd_ref[...] = h0_ref[0]

    if has_head:
        p = jnp.where(t == 0, head_ref[...], p_ref[...])
    else:
        p = p_ref[...]
    row = _iota((c, 1), 0)
    shifted = jnp.where(row == 0, carry_ref[...], pltpu.roll(p, 1, axis=0))
    carry_ref[...] = p[c - 1:c, :]
    xm = p + (shifted - p) * mu_ref[...]

    r = xm[:, 0:w_a]
    k = xm[:, w_a:2 * w_a]
    v = xm[:, 2 * w_a:3 * w_a]
    o = 3 * w_a
    xw = xm[:, o:o + dw]
    xa = xm[:, o + dw:o + dw + da]
    xg = xm[:, o + dw + da:o + dw + da + dg]

    w = -_softplus(-(w0_ref[...] + _dot3(jnp.tanh(xw), wup_ref[...]))) - 0.5
    lw = -jnp.exp(w)
    a = _sigmoid(a0_ref[...] + _dot3(xa, aup_ref[...]))
    g = _dot3(_sigmoid(xg), gup_ref[...])
    kk0 = k * kk_ref[...]
    kmod = k * (1.0 + (a - 1.0) * ka_ref[...])
    if has_head:
        valid = jnp.logical_or(row >= n_null, t > 0)
        lw = jnp.where(valid, lw, 0.0)
        kk0 = jnp.where(valid, kk0, 0.0)
        kmod = jnp.where(valid, kmod, 0.0)

    tri = (_iota((c, c), 1) <= _iota((c, c), 0)).astype(BF16)
    lcum = _dot_exact_lhs(tri, lw)
    lend = lcum[c - 1:c, :]
    e_pos = jnp.exp(lcum)
    e_prev = jnp.exp(lcum - lw)
    e_neg = jnp.exp(-lcum)
    e_end = jnp.exp(lend - lcum)

    lane = _iota((c, LANES), 1)
    rowc = _iota((c, LANES), 0)
    first = lane < HEAD_DIM
    col = jnp.where(first, lane, lane - HEAD_DIM)
    strict = col < rowc
    incl = col <= rowc
    eye = (col == rowc).astype(F32)
    rr = _iota((LANES, LANES), 0)
    cc = _iota((LANES, LANES), 1)
    bd_mask = (rr < HEAD_DIM) == (cc < HEAD_DIM)
    ones_bd = bd_mask.astype(BF16)
    avg_bd = (bd_mask.astype(F32) * (1.0 / HEAD_DIM)).astype(BF16)
    ones_cl = jnp.ones((c, LANES), BF16)

    def bd(x):
        return jnp.concatenate([jnp.where(first, x, 0.0), jnp.where(first, 0.0, x)], axis=0)

    def pack(x1, x2, left):
        if left:
            return jnp.where(first, x1, pltpu.roll(x2, HEAD_DIM, axis=1))
        return jnp.where(first, pltpu.roll(x1, HEAD_DIM, axis=1), x2)

    prs = range(n_pairs)
    sls = [slice(pr * LANES, (pr + 1) * LANES) for pr in prs]
    hs = [hbd_ref[pr] for pr in prs]
    kk0p = [kk0[:, sl] for sl in sls]
    ss = [_dot_exact_rhs(x * x, ones_bd) for x in kk0p]
    kkn = [x / jnp.maximum(jnp.sqrt(s_), 1e-12) for x, s_ in zip(kk0p, ss)]
    bb = [kkn[pr] * a[:, sls[pr]] for pr in prs]
    a_t = [-kkn[pr] * e_prev[:, sls[pr]] for pr in prs]
    b_t = [bb[pr] * e_neg[:, sls[pr]] for pr in prs]
    k_t = [kmod[:, sls[pr]] * e_neg[:, sls[pr]] for pr in prs]
    r_t = [r[:, sls[pr]] * e_pos[:, sls[pr]] for pr in prs]
    b_h = [bb[pr] * e_end[:, sls[pr]] for pr in prs]
    k_h = [kmod[:, sls[pr]] * e_end[:, sls[pr]] for pr in prs]
    vp = [v[:, sl] for sl in sls]
    vbd = [bd(x) for x in vp]

    lhs = [jnp.concatenate([jnp.where(first, a_t[pr], 0.0), jnp.where(first, 0.0, a_t[pr]),
                            jnp.where(first, r_t[pr], 0.0), jnp.where(first, 0.0, r_t[pr])], axis=0) for pr in prs]
    rhs = [jnp.concatenate([b_t[pr], k_t[pr]], axis=0) for pr in prs]
    sc = [_dot3_nt(lhs[pr], rhs[pr]) for pr in prs]
    m_ab = [jnp.where(strict, pack(s_[0:c], s_[c:2 * c], True), 0.0) for s_ in sc]
    m_ak = [jnp.where(strict, pack(s_[0:c], s_[c:2 * c], False), 0.0) for s_ in sc]
    m_rb = [jnp.where(incl, pack(s_[2 * c:3 * c], s_[3 * c:4 * c], True), 0.0) for s_ in sc]
    m_rk = [jnp.where(incl, pack(s_[2 * c:3 * c], s_[3 * c:4 * c], False), 0.0) for s_ in sc]

    same = lambda m: (col >> (m.bit_length() - 1)) == (rowc >> (m.bit_length() - 1))
    tinv = [eye + jnp.where(same(2), x, 0.0) for x in m_ab]
    m = 4
    while m <= c:
        off_mask = jnp.logical_and(same(m), jnp.logical_not(same(m // 2)))
        a_off = [jnp.where(off_mask, x, 0.0) for x in m_ab]
        left = [_dot(tinv[pr], bd(a_off[pr])) for pr in prs]
        tinv = [tinv[pr] + _dot(left[pr], bd(tinv[pr])) for pr in prs]
        m *= 2

    ah = [_dot3(a_t[pr], hs[pr]) for pr in prs]
    akv = [_dot(m_ak[pr], vbd[pr]) for pr in prs]
    rh = [_dot3(r_t[pr], hs[pr]) for pr in prs]
    rkv = [_dot(m_rk[pr], vbd[pr]) for pr in prs]
    u = [_dot3(tinv[pr], bd(ah[pr] + akv[pr])) for pr in prs]
    y = [rh[pr] + rkv[pr] + _dot(m_rb[pr], bd(u[pr])) for pr in prs]

    lend_col = [_dot_exact_rhs(lw[:, sl].T, ones_cl) for sl in sls]
    upd = [_dot(jnp.concatenate([b_h[pr], k_h[pr]], axis=0).T, jnp.concatenate([u[pr], vp[pr]], axis=0))
           for pr in prs]
    h_new = [jnp.where(bd_mask, hs[pr] * jnp.exp(lend_col[pr]) + upd[pr], 0.0) for pr in prs]

    mean = [_dot_exact_rhs(x, avg_bd) for x in y]
    yc = [y[pr] - mean[pr] for pr in prs]
    var = [_dot_exact_rhs(x * x, avg_bd) for x in yc]
    bonus = [_dot_exact_rhs(r[:, sls[pr]] * kmod[:, sls[pr]] * rk_ref[:, sls[pr]], ones_bd) * vp[pr] for pr in prs]
    for pr in prs:
        sl = sls[pr]
        yn = yc[pr] * lax.rsqrt(var[pr] + GN_EPS) * lng_ref[:, sl] + lnb_ref[:, sl]
        o_ref[:, sl] = ((yn + bonus[pr]) * g[:, sl]).astype(o_ref.dtype)
    for pr in prs:
        hbd_ref[pr] = h_new[pr]

    @pl.when(t == nt - 1)
    def _():
        hout_ref[0] = hbd_ref[...]


def _rwkv(proj, head, prev, h0, rw, *, n_seq, n_tiles, row_block, out_block, out_rows, has_head, n_null, pr_cols,
          widths):
    n_pairs = h0.shape[1]
    w_a = n_pairs * LANES
    dw, da, dg = widths
    vec = lambda n: pl.BlockSpec((1, n), lambda s, t: (0, 0))
    mat = lambda a: pl.BlockSpec(a.shape, lambda s, t: (0, 0))
    kern = functools.partial(_rwkv_kernel, n_pairs=n_pairs, has_head=has_head, n_null=n_null,
                             dw=dw, da=da, dg=dg)
    return pl.pallas_call(
        kern,
        grid=(n_seq, n_tiles),
        in_specs=[
            pl.BlockSpec((CHUNK, pr_cols), lambda s, t: (row_block(s, t), 0)),
            pl.BlockSpec((CHUNK, pr_cols), lambda s, t: (0, 0)),
            pl.BlockSpec((1, 1, pr_cols), lambda s, t: (s, 0, 0)),
            pl.BlockSpec((1, n_pairs, LANES, LANES), lambda s, t: (s, 0, 0, 0)),
            vec(pr_cols), vec(w_a), vec(w_a), vec(w_a), vec(w_a), vec(w_a), vec(w_a), vec(w_a),
            mat(rw["w_up"]), mat(rw["a_up"]), mat(rw["g_up"]),
        ],
        out_specs=[
            pl.BlockSpec((CHUNK, w_a), lambda s, t: (out_block(s, t), 0)),
            pl.BlockSpec((1, n_pairs, LANES, LANES), lambda s, t: (s, 0, 0, 0)),
        ],
        out_shape=[
            jax.ShapeDtypeStruct((out_rows, w_a), BF16),
            jax.ShapeDtypeStruct((n_seq, n_pairs, LANES, LANES), F32),
        ],
        scratch_shapes=[pltpu.VMEM((1, pr_cols), F32), pltpu.VMEM((n_pairs, LANES, LANES), F32)],
        compiler_params=_params(("parallel", "arbitrary")),
        name="rwkv_head" if has_head else "rwkv",
    )(proj, head, prev, h0, rw["mu"], rw["w0"], rw["a0"], rw["k_k"], rw["k_a"], rw["r_k"],
      rw["lnx_g"], rw["lnx_b"], rw["w_up"], rw["a_up"], rw["g_up"])


def _sb_kernel(q_ref, km_ref, vm_ref, ke_ref, ve_ref, o_ref, kt_ref, vb_ref, qm_ref, acc_ref, c_ref, *,
               bq, bk, rc, n_main, extra_valid, extra_newest, scale):
    qi = pl.program_id(2)
    sub = bk // LANES

    @pl.when(qi == 0)
    def _():
        def fill(j, carry):
            for s in range(sub):
                blk = km_ref[pl.ds(pl.multiple_of(j * bk + s * LANES, LANES), LANES), :]
                kt_ref[j, :, s * LANES:(s + 1) * LANES] = blk.T.astype(BF16)
            return carry

        lax.fori_loop(0, n_main, fill, 0)
        for s in range(sub):
            kt_ref[n_main, :, s * LANES:(s + 1) * LANES] = ke_ref[s * LANES:(s + 1) * LANES, :].T.astype(BF16)
        vb_ref[0:n_main * bk, :] = vm_ref[...].astype(BF16)
        vb_ref[n_main * bk:(n_main + 1) * bk, :] = ve_ref[...].astype(BF16)

    lane_q = _iota((bq, LANES), 1)
    q = q_ref[...] * (scale * math.log2(math.e))
    qm_ref[0] = jnp.where(lane_q < HEAD_DIM, q, 0.0).astype(BF16)
    qm_ref[1] = jnp.where(lane_q < HEAD_DIM, 0.0, q).astype(BF16)
    later = (_iota((bk, bk), 0) > _iota((bk, bk), 1)).astype(BF16)
    col = _iota((rc, bk), 1)
    row = _iota((rc, bk), 0)
    acc_ref[...] = jnp.zeros_like(acc_ref)
    c_ref[...] = jnp.zeros_like(c_ref)

    def block(j, mask_fn):
        kt = kt_ref[j]
        vblk = vb_ref[pl.ds(pl.multiple_of(j * bk, bk), bk), :]
        units = []
        for r0 in range(0, bq, rc):
            mask = None
            if mask_fn is not None:
                mask = mask_fn(r0)
                if mask is None:
                    continue
            units += [dict(r0=r0, hh=hh, mask=mask) for hh in range(2)]

        def logits(u):
            z = jnp.dot(qm_ref[u["hh"], u["r0"]:u["r0"] + rc, :], kt, preferred_element_type=F32)
            neg_abs = pltpu.bitcast(pltpu.bitcast(z, jnp.uint32) | jnp.uint32(0x80000000), F32)
            nlr = jnp.maximum(z, 0.0) + jnp.log2(1.0 + jnp.exp2(neg_abs))
            if u["mask"] is not None:
                nlr = jnp.where(u["mask"], nlr, 0.0)
            u.update(z=z, nlr=nlr, nlr16=nlr.astype(BF16), rowsum=jnp.sum(nlr, axis=1, keepdims=True))

        def newer(u):
            u["between"] = jnp.dot(u["nlr16"], later, preferred_element_type=F32)

        def weigh(u):
            rows = slice(u["r0"], u["r0"] + rc)
            cb = c_ref[u["hh"], rows, :]
            wgt = jnp.exp2(u["z"] - u["nlr"] - u["between"] - jnp.concatenate([cb] * sub, axis=1))
            if u["mask"] is not None:
                wgt = jnp.where(u["mask"], wgt, 0.0)
            acc_ref[u["hh"], rows, :] += jnp.dot(wgt.astype(BF16), vblk, preferred_element_type=F32)
            c_ref[u["hh"], rows, :] = cb + u["rowsum"]

        stages = (logits, newer, weigh)
        for step in range(len(units) + len(stages) - 1):
            for s_idx, stage in enumerate(stages):
                if 0 <= step - s_idx < len(units):
                    stage(units[step - s_idx])

    def passed():
        return jnp.min(jnp.minimum(c_ref[0], c_ref[1]))

    def sweep(j_first):
        def cond(carry):
            n, c_min = carry
            return jnp.logical_and(n < j_first, c_min < UNDERFLOW_LOG2)

        def body(carry):
            block(j_first - 1 - carry[0], None)
            return carry[0] + 1, passed()

        return lax.while_loop(cond, body, (jnp.int32(0), passed()))[1]

    if extra_newest:
        block(n_main, lambda r0: jnp.logical_and(col < row + r0, col < extra_valid))
        sweep(n_main)
    else:
        j_lo = (qi * bq) // bk
        for d in range(bq // bk - 1, -1, -1):
            block(j_lo + d, lambda r0, d=d: None if d * bk >= r0 + rc - 1 else d * bk + col < row + r0)
        c_min = sweep(j_lo)

        @pl.when(c_min < UNDERFLOW_LOG2)
        def _():
            block(n_main, lambda r0: col < extra_valid)
    o_ref[...] = jnp.where(lane_q < HEAD_DIM, acc_ref[0], acc_ref[1]).astype(o_ref.dtype)


def _sb(q_arr, q_block, k_main, k_block, v_main, v_block, main_rows, k_extra, v_extra, extra_block, *, grid,
        out_rows, out_block, bq, bk, extra_valid, extra_newest, rc=256):
    w_b = k_extra.shape[1]
    n_main = main_rows // bk
    assert extra_newest or bq % bk == 0
    kern = functools.partial(_sb_kernel, bq=bq, bk=bk, rc=min(rc, bq), n_main=n_main, extra_valid=extra_valid,
                             extra_newest=extra_newest, scale=HEAD_DIM ** -0.5)
    return pl.pallas_call(
        kern,
        grid=grid,
        in_specs=[
            pl.BlockSpec((bq, LANES), q_block),
            pl.BlockSpec((main_rows, LANES), k_block),
            pl.BlockSpec((main_rows, LANES), v_block),
            pl.BlockSpec((bk, LANES), extra_block),
            pl.BlockSpec((bk, LANES), extra_block),
        ],
        out_specs=pl.BlockSpec((bq, LANES), out_block),
        out_shape=jax.ShapeDtypeStruct((out_rows, w_b), BF16),
        scratch_shapes=[
            pltpu.VMEM((n_main + 1, LANES, bk), BF16),
            pltpu.VMEM(((n_main + 1) * bk, LANES), BF16),
            pltpu.VMEM((2, bq, LANES), BF16),
            pltpu.VMEM((2, bq, LANES), F32),
            pltpu.VMEM((2, bq, LANES), F32),
        ],
        compiler_params=_params(("parallel", "parallel", "arbitrary")),
        name="stick_breaking_newest" if extra_newest else "stick_breaking",
    )(q_arr, k_main, v_main, k_extra, v_extra)


def _mix_kernel(*refs, n_gc, gw, w_a, alpha, n_exp, n_grp):
    x_ref, oa_ref, ob_ref = refs[0:3]
    ga_refs = refs[3:3 + n_gc]
    gb_refs = refs[3 + n_gc:3 + 2 * n_gc]
    (wb_ref, wo_ref, lig_ref, lib_ref, lmg_ref, lmb_ref, wr_ref, br_ref,
     h_ref, comb_ref, mg_ref) = refs[3 + 2 * n_gc:]
    oa = oa_ref[...]
    ob = ob_ref[...]
    for cidx in range(n_gc):
        cs = slice(cidx * gw, (cidx + 1) * gw)
        pa = jnp.dot(oa, wb_ref[0:w_a, cs], preferred_element_type=F32)
        pb = jnp.dot(ob, wb_ref[w_a:, cs], preferred_element_type=F32)
        mg_ref[:, cs] = (_sigmoid(ga_refs[cidx][...]) * pa + _sigmoid(gb_refs[cidx][...]) * pb).astype(BF16)
    mix = jnp.dot(mg_ref[...], wo_ref[...], preferred_element_type=F32)
    h = _layer_norm(x_ref[...], lig_ref[...], lib_ref[...])
    h1 = _layer_norm(alpha * h + mix, lmg_ref[...], lmb_ref[...])
    h_ref[...] = h1

    lg = _dot3(h1, wr_ref[...]) + br_ref[...]
    lane = _iota(lg.shape, 1).astype(F32)
    neg, big = -1e30, 1e9
    epg = n_exp // n_grp
    is_g = jnp.logical_and(lane >= n_exp, lane < n_exp + n_grp)
    gmax = jnp.max(jnp.where(is_g, lg, neg), axis=1, keepdims=True)
    gsum = jnp.sum(jnp.where(is_g, jnp.exp(jnp.where(is_g, lg, neg) - gmax), 0.0), axis=1, keepdims=True)
    p_g = 1.0 / gsum
    g_idx = jnp.min(jnp.where(jnp.logical_and(is_g, lg == gmax), lane - n_exp, big), axis=1, keepdims=True)
    in_grp = jnp.logical_and(lane >= g_idx * epg, lane < (g_idx + 1.0) * epg)
    el = jnp.where(in_grp, lg, neg)
    m1 = jnp.max(el, axis=1, keepdims=True)
    i1 = jnp.min(jnp.where(jnp.logical_and(in_grp, lg == m1), lane, big), axis=1, keepdims=True)
    el2 = jnp.where(lane == i1, neg, el)
    m2 = jnp.max(el2, axis=1, keepdims=True)
    i2 = jnp.min(jnp.where(jnp.logical_and(el2 == m2, in_grp), lane, big), axis=1, keepdims=True)
    e2 = jnp.exp(m2 - m1)
    w1 = p_g / (1.0 + e2)
    w2 = p_g * e2 / (1.0 + e2)
    comb_ref[...] = (jnp.where(lane == 0.0, i1, 0.0) + jnp.where(lane == 1.0, i2, 0.0)
                     + jnp.where(lane == 2.0, w1, 0.0) + jnp.where(lane == 3.0, w2, 0.0))


def _mix(x_all, o_a, o_b, proj, gate_off, wb, wo, ln_in, ln_mix, wr, br, *, rows, tm, alpha, n_exp, n_grp):
    d = x_all.shape[1]
    w_a = o_a.shape[1]
    gw = math.gcd(gate_off, d)
    n_gc = d // gw
    gspec = lambda blk: pl.BlockSpec((tm, gw), lambda i, blk=blk: (i, blk))
    const = lambda a: pl.BlockSpec(a.shape, lambda i: (0,) * a.ndim, pipeline_mode=pl.Buffered(1))
    kern = functools.partial(_mix_kernel, n_gc=n_gc, gw=gw, w_a=w_a, alpha=alpha, n_exp=n_exp, n_grp=n_grp)
    g0 = gate_off // gw
    args = [x_all, o_a, o_b] + [proj] * (2 * n_gc) + [wb, wo, ln_in[0], ln_in[1], ln_mix[0], ln_mix[1], wr, br]
    in_specs = ([pl.BlockSpec((tm, d), lambda i: (i, 0)),
                 pl.BlockSpec((tm, w_a), lambda i: (i, 0)),
                 pl.BlockSpec((tm, o_b.shape[1]), lambda i: (i, 0))]
                + [gspec(g0 + cidx) for cidx in range(n_gc)]
                + [gspec(g0 + n_gc + cidx) for cidx in range(n_gc)]
                + [const(a) for a in args[3 + 2 * n_gc:]])
    return pl.pallas_call(
        kern,
        grid=(rows // tm,),
        in_specs=in_specs,
        out_specs=[pl.BlockSpec((tm, d), lambda i: (i, 0)), pl.BlockSpec((tm, LANES), lambda i: (i, 0))],
        out_shape=[jax.ShapeDtypeStruct((rows, d), F32), jax.ShapeDtypeStruct((rows, LANES), F32)],
        scratch_shapes=[pltpu.VMEM((tm, d), BF16)],
        compiler_params=_params(("parallel",)),
        name="mix_out_router",
    )(*args)


def _gather_rows(src_hbm, idx_ref, base, n, dst, sem):
    def issue(r, carry):
        pltpu.make_async_copy(src_hbm.at[pl.ds(idx_ref[base + r], 1)], dst.at[pl.ds(r, 1)], sem).start()
        return carry

    lax.fori_loop(0, n, issue, 0, unroll=8)


def _wait_rows(src_hbm, n, dst, sem):
    pltpu.make_async_copy(src_hbm.at[pl.ds(0, n)], dst, sem).wait()


def _moe_ffn_kernel(te_ref, nu_ref, rt_ref, x_hbm, wg_ref, wu_ref, wd_ref, y_ref, xbuf, sem, wgb, wub, wdb, *, tm):
    t = pl.program_id(0)
    n_used = nu_ref[0]

    @pl.when(jnp.logical_and(t == 0, n_used > 0))
    def _():
        _gather_rows(x_hbm, rt_ref, 0, tm, xbuf.at[0], sem.at[0])

    @pl.when(t + 1 < n_used)
    def _():
        nxt = (t + 1) % 2
        _gather_rows(x_hbm, rt_ref, (t + 1) * tm, tm, xbuf.at[nxt], sem.at[nxt])

    @pl.when(t < n_used)
    def _():
        slot = t % 2
        _wait_rows(x_hbm, tm, xbuf.at[slot], sem.at[slot])

        @pl.when(jnp.logical_or(t == 0, te_ref[t] != te_ref[jnp.maximum(t - 1, 0)]))
        def _():
            wgb[...] = wg_ref[0].astype(BF16)
            wub[...] = wu_ref[0].astype(BF16)
            wdb[...] = wd_ref[0].astype(BF16)

        x = xbuf[slot].astype(BF16)
        hg = jnp.dot(x, wgb[...], preferred_element_type=F32)
        hu = jnp.dot(x, wub[...], preferred_element_type=F32)
        h = hg * _sigmoid(hg) * hu
        y_ref[...] = jnp.dot(h.astype(BF16), wdb[...], preferred_element_type=F32)

    @pl.when(t >= n_used)
    def _():
        y_ref[...] = jnp.zeros_like(y_ref)


def _moe_ffn(h1, tile_expert, n_used, row_token, wg, wu, wd, *, tm):
    d = h1.shape[1]
    n_exp, _, d_e = wg.shape
    n_tiles = tile_expert.shape[0]
    wspec = lambda shape: pl.BlockSpec((1,) + shape, lambda t, te, nu, rt: (te[t], 0, 0))
    return pl.pallas_call(
        functools.partial(_moe_ffn_kernel, tm=tm),
        grid_spec=pltpu.PrefetchScalarGridSpec(
            num_scalar_prefetch=3,
            grid=(n_tiles,),
            in_specs=[pl.BlockSpec(memory_space=pl.ANY), wspec((d, d_e)), wspec((d, d_e)), wspec((d_e, d))],
            out_specs=pl.BlockSpec((tm, d), lambda t, te, nu, rt: (t, 0)),
            scratch_shapes=[pltpu.VMEM((2, tm, d), F32), pltpu.SemaphoreType.DMA((2,)),
                            pltpu.VMEM((d, d_e), BF16), pltpu.VMEM((d, d_e), BF16), pltpu.VMEM((d_e, d), BF16)],
        ),
        out_shape=jax.ShapeDtypeStruct((n_tiles * tm, d), F32),
        compiler_params=pltpu.CompilerParams(dimension_semantics=("arbitrary",), vmem_limit_bytes=VMEM_LIMIT,
                                             disable_bounds_checks=True),
        name="moe_ffn",
    )(tile_expert, n_used, row_token, h1, wg, wu, wd)


def _moe_combine_kernel(pos_ref, y_hbm, x_ref, r_ref, g_ref, b_ref, op_ref, os_ref, buf, sem, *, tm, tiles_p, alpha):
    i = pl.program_id(0)
    n = pl.num_programs(0)

    def gather(tile, slot):
        def issue(r, carry):
            for k in range(TOP_K):
                p = pos_ref[TOP_K * (tile * tm + r) + k]
                pltpu.make_async_copy(y_hbm.at[pl.ds(p, 1)], buf.at[slot, pl.ds(k * tm + r, 1)], sem.at[slot]).start()
            return carry

        lax.fori_loop(0, tm, issue, 0, unroll=4)

    @pl.when(i == 0)
    def _():
        gather(0, 0)

    @pl.when(i + 1 < n)
    def _():
        gather(i + 1, (i + 1) % 2)

    slot = i % 2
    _wait_rows(y_hbm, TOP_K * tm, buf.at[slot], sem.at[slot])
    rec = r_ref[...]
    w1 = rec[:, 2:3]
    w2 = rec[:, 3:4]
    ffn = w1 * buf[slot, 0:tm, :] + w2 * buf[slot, tm:2 * tm, :]
    out = _layer_norm(alpha * x_ref[...] + ffn, g_ref[...], b_ref[...])

    @pl.when(i < tiles_p)
    def _():
        op_ref[...] = out

    @pl.when(i >= tiles_p)
    def _():
        os_ref[...] = out


def _moe_combine(pos, y_sorted, h1, route, ln_g, ln_b, *, n_p, tm, alpha):
    rows, d = h1.shape
    tiles_p = n_p // tm
    return pl.pallas_call(
        functools.partial(_moe_combine_kernel, tm=tm, tiles_p=tiles_p, alpha=alpha),
        grid_spec=pltpu.PrefetchScalarGridSpec(
            num_scalar_prefetch=1,
            grid=(rows // tm,),
            in_specs=[
                pl.BlockSpec(memory_space=pl.ANY),
                pl.BlockSpec((tm, d), lambda i, pos: (i, 0)),
                pl.BlockSpec((tm, LANES), lambda i, pos: (i, 0)),
                pl.BlockSpec((1, d), lambda i, pos: (0, 0)),
                pl.BlockSpec((1, d), lambda i, pos: (0, 0)),
            ],
            out_specs=[pl.BlockSpec((tm, d), lambda i, pos: (jnp.minimum(i, tiles_p - 1), 0)),
                       pl.BlockSpec((tm, d), lambda i, pos: (jnp.maximum(i - tiles_p, 0), 0))],
            scratch_shapes=[pltpu.VMEM((2, TOP_K * tm, d), F32), pltpu.SemaphoreType.DMA((2,))],
        ),
        out_shape=[jax.ShapeDtypeStruct((n_p, d), F32), jax.ShapeDtypeStruct((rows - n_p, d), F32)],
        compiler_params=pltpu.CompilerParams(dimension_semantics=("arbitrary",), vmem_limit_bytes=VMEM_LIMIT,
                                             disable_bounds_checks=True),
        name="moe_combine",
    )(pos, y_sorted, h1, route, ln_g, ln_b)


def _dispatch_plan(route, n_exp, tm):
    n_assign = route.shape[0] * TOP_K
    flat_e = route[:, :TOP_K].astype(jnp.int32).reshape(-1)
    order = jnp.argsort(flat_e, stable=True).astype(jnp.int32)
    sorted_e = flat_e[order]
    counts = jnp.zeros((n_exp,), jnp.int32).at[flat_e].add(1)
    padded = (counts + tm - 1) // tm * tm
    ends_p = jnp.cumsum(padded)
    starts_p = ends_p - padded
    starts = jnp.cumsum(counts) - counts
    pos_sorted = jnp.arange(n_assign, dtype=jnp.int32) - starts[sorted_e] + starts_p[sorted_e]
    n_tiles = (n_assign + n_exp * (tm - 1)) // tm + 1
    row_token = jnp.zeros((n_tiles * tm,), jnp.int32).at[pos_sorted].set(order // TOP_K)
    pos = jnp.zeros((n_assign,), jnp.int32).at[order].set(pos_sorted)
    n_used = ends_p[-1] // tm
    tile_idx = jnp.minimum(jnp.arange(n_tiles, dtype=jnp.int32), n_used - 1)
    tile_expert = jnp.minimum(jnp.searchsorted(ends_p, tile_idx * tm, side="right"), n_exp - 1).astype(jnp.int32)
    return tile_expert, n_used.reshape(1).astype(jnp.int32), row_token, pos


def _pad_cols(a, n):
    return jnp.pad(a, [(0, 0)] * (a.ndim - 1) + [(0, n - a.shape[-1])])


def kernel(x_prompt, x_sample, cache_sb_k, cache_sb_v, state_rwkv, state_rwkv_shift, meta_tokens, ln_in_g, ln_in_b, w_in, rwkv_mu, rwkv_w0, rwkv_w_up, rwkv_a0, rwkv_a_up, rwkv_g_up, rwkv_k_k, rwkv_k_a, rwkv_r_k, rwkv_lnx_g, rwkv_lnx_b, w_branch, w_out, ln_mix_g, ln_mix_b, router_group_w, router_group_b, router_expert_w, router_expert_b, moe_w_gate, moe_w_up, moe_w_down, ln_ffn_g, ln_ffn_b):
    depth = w_in.shape[0]
    assert depth == 1, "single-layer step only"
    nb, seq, d = x_prompt.shape
    db, ds = x_sample.shape[:2]
    n_meta = meta_tokens.shape[0]
    past = cache_sb_k.shape[2]
    w_a = rwkv_w0.shape[1]
    r_d, r_a, r_g = rwkv_w_up.shape[1], rwkv_a_up.shape[1], rwkv_g_up.shape[1]
    w_shift = 3 * w_a + r_d + r_a + r_g
    w_b = (w_in.shape[2] - w_shift - 2 * d) // 3
    h_a, h_b = w_a // HEAD_DIM, w_b // HEAD_DIM
    n_grp = router_group_w.shape[2]
    n_exp = router_expert_w.shape[2]
    alpha = (2 * depth) ** 0.25
    assert w_a % LANES == 0 and w_b % LANES == 0 and seq % 256 == 0 and ds == CHUNK and n_meta <= CHUNK
    assert n_exp + n_grp <= LANES

    dw, da, dg = (_round_up(n, LANES) for n in (r_d, r_a, r_g))
    pr_cols = 3 * w_a + dw + da + dg
    wi = w_in[0]
    s0 = 3 * w_a
    segs = [wi[:, :s0], _pad_cols(wi[:, s0:s0 + r_d], dw), _pad_cols(wi[:, s0 + r_d:s0 + r_d + r_a], da),
            _pad_cols(wi[:, s0 + r_d + r_a:w_shift], dg), wi[:, w_shift:]]
    w_pad = jnp.concatenate(segs, axis=1).astype(BF16)
    p_cols = w_pad.shape[1]
    q_off, k_off, v_off, gate_off = pr_cols, pr_cols + w_b, pr_cols + 2 * w_b, pr_cols + 3 * w_b

    def to_padded_rwkv_cols(a):
        return jnp.concatenate([a[..., :s0], _pad_cols(a[..., s0:s0 + r_d], dw),
                                _pad_cols(a[..., s0 + r_d:s0 + r_d + r_a], da),
                                _pad_cols(a[..., s0 + r_d + r_a:], dg)], axis=-1)

    def from_padded_rwkv_cols(a):
        return jnp.concatenate([a[..., :s0], a[..., s0:s0 + r_d], a[..., s0 + dw:s0 + dw + r_a],
                                a[..., s0 + dw + da:s0 + dw + da + r_g]], axis=-1)

    n_p, n_s = nb * seq, db * ds
    rows_main = n_p + n_s
    tm1 = _pick_tile(rows_main, (512, 256, 128))
    rows_all = _round_up(rows_main + n_meta, tm1)
    x_all = jnp.concatenate([x_prompt.reshape(n_p, d), x_sample.reshape(n_s, d), meta_tokens.astype(F32),
                             jnp.zeros((rows_all - rows_main - n_meta, d), F32)], axis=0)
    tn1 = _pick_tile(p_cols, (1792, 512, 256, 128))
    proj = _inproj(x_all, ln_in_g[None], ln_in_b[None], w_pad, tm1, tn1)
    proj_meta = proj[rows_main:rows_main + n_meta]

    row2 = lambda a: a.reshape(1, -1)
    rw = dict(mu=to_padded_rwkv_cols(rwkv_mu[0])[None], w0=row2(rwkv_w0[0]), a0=row2(rwkv_a0[0]),
              k_k=row2(rwkv_k_k[0]), k_a=row2(rwkv_k_a[0]), r_k=row2(rwkv_r_k[0]),
              lnx_g=row2(rwkv_lnx_g[0]), lnx_b=row2(rwkv_lnx_b[0]),
              w_up=jnp.pad(rwkv_w_up[0], ((0, dw - r_d), (0, 0))),
              a_up=jnp.pad(rwkv_a_up[0], ((0, da - r_a), (0, 0))),
              g_up=jnp.pad(rwkv_g_up[0], ((0, dg - r_g), (0, 0))))
    n_pairs_a = w_a // LANES
    n_null = CHUNK - n_meta
    head = jnp.concatenate([jnp.zeros((n_null, pr_cols), F32), proj_meta[:, :pr_cols]], axis=0)
    tiles_p = seq // CHUNK
    o_a_p, hbd_p = _rwkv(proj, head, jnp.zeros((nb, 1, pr_cols), F32),
                         jnp.zeros((nb, n_pairs_a, LANES, LANES), F32), rw,
                         n_seq=nb, n_tiles=tiles_p + 1,
                         row_block=lambda s, t: s * tiles_p + jnp.maximum(t - 1, 0),
                         out_block=lambda s, t: s * tiles_p + jnp.maximum(t - 1, 0), out_rows=n_p, has_head=True, n_null=n_null, pr_cols=pr_cols, widths=(dw, da, dg))

    def to_hbd(state):
        n = state.shape[0]
        hkv = jnp.swapaxes(state, -1, -2).reshape(n, n_pairs_a, 2, HEAD_DIM, HEAD_DIM)
        z = jnp.zeros_like(hkv[:, :, 0])
        top = jnp.concatenate([hkv[:, :, 0], z], axis=-1)
        bot = jnp.concatenate([z, hkv[:, :, 1]], axis=-1)
        return jnp.concatenate([top, bot], axis=-2)

    def from_hbd(hbd):
        h0 = hbd[:, :, :HEAD_DIM, :HEAD_DIM]
        h1 = hbd[:, :, HEAD_DIM:, HEAD_DIM:]
        hkv = jnp.stack([h0, h1], axis=2).reshape(hbd.shape[0], h_a, HEAD_DIM, HEAD_DIM)
        return jnp.swapaxes(hkv, -1, -2)

    base_s = n_p // CHUNK
    o_a_s, hbd_s = _rwkv(proj, head, to_padded_rwkv_cols(state_rwkv_shift[0]), to_hbd(state_rwkv[0]), rw,
                         n_seq=db, n_tiles=ds // CHUNK,
                         row_block=lambda s, t: base_s + s * (ds // CHUNK) + t,
                         out_block=lambda s, t: s * (ds // CHUNK) + t, out_rows=n_s, has_head=False, n_null=0, pr_cols=pr_cols, widths=(dw, da, dg))
    o_a = jnp.concatenate([o_a_p, o_a_s], axis=0)

    bk = 256
    bq_p = _pick_tile(seq, (512, 256))
    assert past % bk == 0 and n_meta <= bk and ds <= bk
    n_pairs_b = w_b // LANES
    qc, kc, vc = q_off // LANES, k_off // LANES, v_off // LANES
    k_meta, v_meta = proj_meta[:, k_off:k_off + w_b], proj_meta[:, v_off:v_off + w_b]
    pad_rows = lambda a: jnp.pad(a, ((0, bk - a.shape[0]), (0, 0)))
    qt_p = seq // bq_p
    o_b_p = _sb(proj, lambda b, p, i: (b * qt_p + i, qc + p),
                proj, lambda b, p, i: (b, kc + p), proj, lambda b, p, i: (b, vc + p), seq,
                pad_rows(k_meta), pad_rows(v_meta), lambda b, p, i: (0, p),
                grid=(nb, n_pairs_b, qt_p), out_rows=n_p, out_block=lambda b, p, i: (b * qt_p + i, p),
                bq=bq_p, bk=bk, extra_valid=n_meta, extra_newest=False)

    new = lambda off: proj[n_p:rows_main, off:off + w_b].reshape(db, ds, w_b)
    k_new, v_new = new(k_off), new(v_off)
    pad_seq = lambda a: jnp.pad(a, ((0, 0), (0, bk - ds), (0, 0))).reshape(db * bk, w_b)
    o_b_s = _sb(proj, lambda b, p, i: (n_p // ds + b, qc + p),
                cache_sb_k[0].reshape(db * past, w_b), lambda b, p, i: (b, p),
                cache_sb_v[0].reshape(db * past, w_b), lambda b, p, i: (b, p), past,
                pad_seq(k_new), pad_seq(v_new), lambda b, p, i: (b, p),
                grid=(db, n_pairs_b, 1), out_rows=n_s, out_block=lambda b, p, i: (b, p),
                bq=ds, bk=bk, extra_valid=ds, extra_newest=True)
    o_b = jnp.concatenate([o_b_p, o_b_s], axis=0)

    wr = jnp.concatenate([router_expert_w[0], router_group_w[0]], axis=1)
    wr = _pad_cols(wr, LANES)
    br = _pad_cols(jnp.concatenate([router_expert_b[0], router_group_b[0]])[None], LANES)
    tm4 = _pick_tile(rows_main, (256, 128))
    h1, route = _mix(x_all, o_a, o_b, proj, gate_off, w_branch[0].astype(BF16), w_out[0].astype(BF16),
                     (ln_in_g[None], ln_in_b[None]), (ln_mix_g, ln_mix_b), wr, br,
                     rows=rows_main, tm=tm4, alpha=alpha, n_exp=n_exp, n_grp=n_grp)

    tm5 = _pick_tile(n_s, (256, 128))
    assert n_p % tm5 == 0
    tile_expert, n_used, row_token, pos = _dispatch_plan(route, n_exp, tm5)
    y_sorted = _moe_ffn(h1, tile_expert, n_used, row_token, moe_w_gate[0], moe_w_up[0], moe_w_down[0], tm=tm5)
    y_p, y_s = _moe_combine(pos, y_sorted, h1, route, ln_ffn_g, ln_ffn_b, n_p=n_p, tm=tm5, alpha=alpha)

    y_p = y_p.reshape(nb, seq, d)
    y_s = y_s.reshape(db, ds, d)
    heads = lambda a, n, t: a.reshape(1, n, t, h_b, HEAD_DIM)
    with_meta = lambda m, off: jnp.concatenate(
        [jnp.broadcast_to(m[None], (nb, n_meta, w_b)), proj[:n_p, off:off + w_b].reshape(nb, seq, w_b)], axis=1)
    sb_k_p = heads(with_meta(k_meta, k_off), nb, n_meta + seq)
    sb_v_p = heads(with_meta(v_meta, v_off), nb, n_meta + seq)
    last_p = proj[seq - 1:n_p:seq, :pr_cols]
    last_s = proj[n_p + ds - 1:rows_main:ds, :pr_cols]
    shift_p = from_padded_rwkv_cols(last_p)[None, :, None, :]
    shift_s = from_padded_rwkv_cols(last_s)[None, :, None, :]
    return (y_p, y_s, sb_k_p, sb_v_p, from_hbd(hbd_p)[None], shift_p,
            heads(k_new, db, ds), heads(v_new, db, ds), from_hbd(hbd_s)[None], shift_s)
```

```python
import functools
import math

import jax
import jax.numpy as jnp
from jax import lax
from jax.experimental import pallas as pl
from jax.experimental.pallas import tpu as pltpu

F32 = jnp.float32
BF16 = jnp.bfloat16

HEAD_DIM = 64
LANES = 128
CHUNK = 64
TOP_K = 2
LN_EPS = 1e-5
GN_EPS = 64e-5
UNDERFLOW_LOG2 = 150.0
VMEM_LIMIT = 56 * 1024 * 1024


def _round_up(n, m):
    return -(-n // m) * m


def _pick_tile(n, candidates):
    for c in candidates:
        if n % c == 0:
            return c
    raise ValueError(f"no tile in {candidates} divides {n}")


def _params(sem, vmem=VMEM_LIMIT):
    return pltpu.CompilerParams(dimension_semantics=sem, vmem_limit_bytes=vmem)


def _dot(a, b):
    return jnp.dot(a.astype(BF16), b.astype(BF16), preferred_element_type=F32)


def _split2(x):
    hi = x.astype(BF16)
    lo = (x - hi.astype(F32)).astype(BF16)
    return hi, lo


def _split3(x):
    hi = x.astype(BF16)
    r1 = x - hi.astype(F32)
    mid = r1.astype(BF16)
    lo = (r1 - mid.astype(F32)).astype(BF16)
    return hi, mid, lo


def _dot3(a, b):
    ah, al = _split2(a)
    bh, bl = _split2(b)
    d = lambda x, y: jnp.dot(x, y, preferred_element_type=F32)
    return d(ah, bh) + (d(ah, bl) + d(al, bh))


def _dot3_nt(a, b):
    ah, al = _split2(a)
    bh, bl = _split2(b)
    d = lambda x, y: lax.dot_general(x, y, (((1,), (1,)), ((), ())), preferred_element_type=F32)
    return d(ah, bh) + (d(ah, bl) + d(al, bh))


def _dot_exact_rhs(a, b_bf16):
    hi, lo = _split2(a)
    d = lambda x: jnp.dot(x, b_bf16, preferred_element_type=F32)
    return d(hi) + d(lo)


def _dot_exact_lhs(a_bf16, b):
    hi, mid, lo = _split3(b)
    d = lambda x: jnp.dot(a_bf16, x, preferred_element_type=F32)
    return d(hi) + (d(mid) + d(lo))


def _layer_norm(x, g, b):
    mu = jnp.mean(x, axis=-1, keepdims=True)
    xc = x - mu
    var = jnp.mean(xc * xc, axis=-1, keepdims=True)
    return xc * lax.rsqrt(var + LN_EPS) * g + b


def _sigmoid(x):
    return 1.0 / (1.0 + jnp.exp(-x))


def _softplus(x):
    return jnp.maximum(x, 0.0) + jnp.log(1.0 + jnp.exp(-jnp.abs(x)))


def _iota(shape, dim):
    return lax.broadcasted_iota(jnp.int32, shape, dim)


def _inproj_kernel(xp_ref, xt_ref, g_ref, b_ref, w_ref, o_ref, xn_ref, *, tiles_p):
    @pl.when(pl.program_id(1) == 0)
    def _():
        x = jnp.where(pl.program_id(0) < tiles_p, xp_ref[...], xt_ref[...])
        xn_ref[...] = _layer_norm(x, g_ref[...], b_ref[...]).astype(BF16)

    o_ref[...] = jnp.dot(xn_ref[...], w_ref[...], preferred_element_type=F32)


def _two_part_rows(tm, d, tiles_p):
    return [pl.BlockSpec((tm, d), lambda i, *_: (jnp.minimum(i, tiles_p - 1), 0)),
            pl.BlockSpec((tm, d), lambda i, *_: (jnp.maximum(i - tiles_p, 0), 0))]


def _inproj(x_p, x_t, ln_g, ln_b, w_pad, tm, tn):
    d = x_p.shape[1]
    rows = x_p.shape[0] + x_t.shape[0]
    tiles_p = x_p.shape[0] // tm
    cols = w_pad.shape[1]
    return pl.pallas_call(
        functools.partial(_inproj_kernel, tiles_p=tiles_p),
        grid=(rows // tm, cols // tn),
        in_specs=_two_part_rows(tm, d, tiles_p) + [
            pl.BlockSpec((1, d), lambda i, j: (0, 0)),
            pl.BlockSpec((1, d), lambda i, j: (0, 0)),
            pl.BlockSpec((d, tn), lambda i, j: (0, j)),
        ],
        out_specs=pl.BlockSpec((tm, tn), lambda i, j: (i, j)),
        out_shape=jax.ShapeDtypeStruct((rows, cols), F32),
        scratch_shapes=[pltpu.VMEM((tm, d), BF16)],
        compiler_params=_params(("parallel", "arbitrary")),
        name="inproj",
    )(x_p, x_t, ln_g, ln_b, w_pad)


def _rwkv_kernel(p_ref, head_ref, prev_ref, h0_ref, mu_ref, w0_ref, a0_ref, kk_ref, ka_ref, rk_ref,
                 lng_ref, lnb_ref, wup_ref, aup_ref, gup_ref,
                 o_ref, hout_ref, carry_ref, hbd_ref, *, n_pairs, has_head, n_null, dw, da, dg):
    t = pl.program_id(1)
    nt = pl.num_programs(1)
    w_a = n_pairs * LANES
    c = CHUNK

    @pl.when(t == 0)
    def _():
        carry_ref[...] = prev_ref[0]
        hbd_ref[...] = h0_ref[0]

    if has_head:
        p = jnp.where(t == 0, head_ref[...], p_ref[...])
    else:
        p = p_ref[...]
    row = _iota((c, 1), 0)
    shifted = jnp.where(row == 0, carry_ref[...], pltpu.roll(p, 1, axis=0))
    carry_ref[...] = p[c - 1:c, :]
    xm = p + (shifted - p) * mu_ref[...]

    r = xm[:, 0:w_a]
    k = xm[:, w_a:2 * w_a]
    v = xm[:, 2 * w_a:3 * w_a]
    o = 3 * w_a
    xw = xm[:, o:o + dw]
    xa = xm[:, o + dw:o + dw + da]
    xg = xm[:, o + dw + da:o + dw + da + dg]

    w = -_softplus(-(w0_ref[...] + _dot3(jnp.tanh(xw), wup_ref[...]))) - 0.5
    lw = -jnp.exp(w)
    a = _sigmoid(a0_ref[...] + _dot3(xa, aup_ref[...]))
    g = _dot3(_sigmoid(xg), gup_ref[...])
    kk0 = k * kk_ref[...]
    kmod = k * (1.0 + (a - 1.0) * ka_ref[...])
    if has_head:
        valid = jnp.logical_or(row >= n_null, t > 0)
        lw = jnp.where(valid, lw, 0.0)
        kk0 = jnp.where(valid, kk0, 0.0)
        kmod = jnp.where(valid, kmod, 0.0)

    tri = (_iota((c, c), 1) <= _iota((c, c), 0)).astype(BF16)
    lcum = _dot_exact_lhs(tri, lw)
    lend = lcum[c - 1:c, :]
    e_pos = jnp.exp(lcum)
    e_prev = jnp.exp(lcum - lw)
    e_neg = jnp.exp(-lcum)
    e_end = jnp.exp(lend - lcum)

    lane = _iota((c, LANES), 1)
    rowc = _iota((c, LANES), 0)
    first = lane < HEAD_DIM
    col = jnp.where(first, lane, lane - HEAD_DIM)
    strict = col < rowc
    incl = col <= rowc
    eye = (col == rowc).astype(F32)
    rr = _iota((LANES, LANES), 0)
    cc = _iota((LANES, LANES), 1)
    bd_mask = (rr < HEAD_DIM) == (cc < HEAD_DIM)
    ones_bd = bd_mask.astype(BF16)
    avg_bd = (bd_mask.astype(F32) * (1.0 / HEAD_DIM)).astype(BF16)
    ones_cl = jnp.ones((c, LANES), BF16)

    def bd(x):
        return jnp.concatenate([jnp.where(first, x, 0.0), jnp.where(first, 0.0, x)], axis=0)

    def pack(x1, x2, left):
        if left:
            return jnp.where(first, x1, pltpu.roll(x2, HEAD_DIM, axis=1))
        return jnp.where(first, pltpu.roll(x1, HEAD_DIM, axis=1), x2)

    prs = range(n_pairs)
    sls = [slice(pr * LANES, (pr + 1) * LANES) for pr in prs]
    hs = [hbd_ref[pr] for pr in prs]
    kk0p = [kk0[:, sl] for sl in sls]
    ss = [_dot_exact_rhs(x * x, ones_bd) for x in kk0p]
    kkn = [x / jnp.maximum(jnp.sqrt(s_), 1e-12) for x, s_ in zip(kk0p, ss)]
    bb = [kkn[pr] * a[:, sls[pr]] for pr in prs]
    a_t = [-kkn[pr] * e_prev[:, sls[pr]] for pr in prs]
    b_t = [bb[pr] * e_neg[:, sls[pr]] for pr in prs]
    k_t = [kmod[:, sls[pr]] * e_neg[:, sls[pr]] for pr in prs]
    r_t = [r[:, sls[pr]] * e_pos[:, sls[pr]] for pr in prs]
    b_h = [bb[pr] * e_end[:, sls[pr]] for pr in prs]
    k_h = [kmod[:, sls[pr]] * e_end[:, sls[pr]] for pr in prs]
    vp = [v[:, sl] for sl in sls]
    vbd = [bd(x) for x in vp]

    lhs = [jnp.concatenate([jnp.where(first, a_t[pr], 0.0), jnp.where(first, 0.0, a_t[pr]),
                            jnp.where(first, r_t[pr], 0.0), jnp.where(first, 0.0, r_t[pr])], axis=0) for pr in prs]
    rhs = [jnp.concatenate([b_t[pr], k_t[pr]], axis=0) for pr in prs]
    sc = [_dot3_nt(lhs[pr], rhs[pr]) for pr in prs]
    m_ab = [jnp.where(strict, pack(s_[0:c], s_[c:2 * c], True), 0.0) for s_ in sc]
    m_ak = [jnp.where(strict, pack(s_[0:c], s_[c:2 * c], False), 0.0) for s_ in sc]
    m_rb = [jnp.where(incl, pack(s_[2 * c:3 * c], s_[3 * c:4 * c], True), 0.0) for s_ in sc]
    m_rk = [jnp.where(incl, pack(s_[2 * c:3 * c], s_[3 * c:4 * c], False), 0.0) for s_ in sc]

    same = lambda m: (col >> (m.bit_length() - 1)) == (rowc >> (m.bit_length() - 1))
    tinv = [eye + jnp.where(same(2), x, 0.0) for x in m_ab]
    m = 4
    while m <= c:
        off_mask = jnp.logical_and(same(m), jnp.logical_not(same(m // 2)))
        a_off = [jnp.where(off_mask, x, 0.0) for x in m_ab]
        left = [_dot(tinv[pr], bd(a_off[pr])) for pr in prs]
        tinv = [tinv[pr] + _dot(left[pr], bd(tinv[pr])) for pr in prs]
        m *= 2

    ah = [_dot3(a_t[pr], hs[pr]) for pr in prs]
    akv = [_dot(m_ak[pr], vbd[pr]) for pr in prs]
    rh = [_dot3(r_t[pr], hs[pr]) for pr in prs]
    rkv = [_dot(m_rk[pr], vbd[pr]) for pr in prs]
    u = [_dot3(tinv[pr], bd(ah[pr] + akv[pr])) for pr in prs]
    y = [rh[pr] + rkv[pr] + _dot(m_rb[pr], bd(u[pr])) for pr in prs]

    lend_col = [_dot_exact_rhs(lw[:, sl].T, ones_cl) for sl in sls]
    upd = [_dot(jnp.concatenate([b_h[pr], k_h[pr]], axis=0).T, jnp.concatenate([u[pr], vp[pr]], axis=0))
           for pr in prs]
    h_new = [jnp.where(bd_mask, hs[pr] * jnp.exp(lend_col[pr]) + upd[pr], 0.0) for pr in prs]

    mean = [_dot_exact_rhs(x, avg_bd) for x in y]
    yc = [y[pr] - mean[pr] for pr in prs]
    var = [_dot_exact_rhs(x * x, avg_bd) for x in yc]
    bonus = [_dot_exact_rhs(r[:, sls[pr]] * kmod[:, sls[pr]] * rk_ref[:, sls[pr]], ones_bd) * vp[pr] for pr in prs]
    for pr in prs:
        sl = sls[pr]
        yn = yc[pr] * lax.rsqrt(var[pr] + GN_EPS) * lng_ref[:, sl] + lnb_ref[:, sl]
        o_ref[:, sl] = ((yn + bonus[pr]) * g[:, sl]).astype(o_ref.dtype)
    for pr in prs:
        hbd_ref[pr] = h_new[pr]

    @pl.when(t == nt - 1)
    def _():
        hout_ref[0] = hbd_ref[...]


def _rwkv(proj, head, prev, h0, rw, *, n_seq, n_tiles, row_block, out_block, out_rows, has_head, n_null, pr_cols,
          widths):
    n_pairs = h0.shape[1]
    w_a = n_pairs * LANES
    dw, da, dg = widths
    vec = lambda n: pl.BlockSpec((1, n), lambda s, t: (0, 0))
    mat = lambda a: pl.BlockSpec(a.shape, lambda s, t: (0, 0))
    kern = functools.partial(_rwkv_kernel, n_pairs=n_pairs, has_head=has_head, n_null=n_null,
                             dw=dw, da=da, dg=dg)
    return pl.pallas_call(
        kern,
        grid=(n_seq, n_tiles),
        in_specs=[
            pl.BlockSpec((CHUNK, pr_cols), lambda s, t: (row_block(s, t), 0)),
            pl.BlockSpec((CHUNK, pr_cols), lambda s, t: (0, 0)),
            pl.BlockSpec((1, 1, pr_cols), lambda s, t: (s, 0, 0)),
            pl.BlockSpec((1, n_pairs, LANES, LANES), lambda s, t: (s, 0, 0, 0)),
            vec(pr_cols), vec(w_a), vec(w_a), vec(w_a), vec(w_a), vec(w_a), vec(w_a), vec(w_a),
            mat(rw["w_up"]), mat(rw["a_up"]), mat(rw["g_up"]),
        ],
        out_specs=[
            pl.BlockSpec((CHUNK, w_a), lambda s, t: (out_block(s, t), 0)),
            pl.BlockSpec((1, n_pairs, LANES, LANES), lambda s, t: (s, 0, 0, 0)),
        ],
        out_shape=[
            jax.ShapeDtypeStruct((out_rows, w_a), BF16),
            jax.ShapeDtypeStruct((n_seq, n_pairs, LANES, LANES), F32),
        ],
        scratch_shapes=[pltpu.VMEM((1, pr_cols), F32), pltpu.VMEM((n_pairs, LANES, LANES), F32)],
        compiler_params=_params(("parallel", "arbitrary")),
        name="rwkv_head" if has_head else "rwkv",
    )(proj, head, prev, h0, rw["mu"], rw["w0"], rw["a0"], rw["k_k"], rw["k_a"], rw["r_k"],
      rw["lnx_g"], rw["lnx_b"], rw["w_up"], rw["a_up"], rw["g_up"])


def _sb_kernel(q_ref, km_ref, vm_ref, ke_ref, ve_ref, o_ref, cmin_ref, kt_ref, vb_ref, qm_ref, acc_ref, c_ref, *,
               bq, bk, rc, n_main, extra_valid, extra_newest, scale):
    qi = pl.program_id(2)
    sub = bk // LANES

    @pl.when(qi == 0)
    def _():
        def fill(j, carry):
            for s in range(sub):
                blk = km_ref[pl.ds(pl.multiple_of(j * bk + s * LANES, LANES), LANES), :]
                kt_ref[j, :, s * LANES:(s + 1) * LANES] = blk.T.astype(BF16)
            return carry

        lax.fori_loop(0, n_main, fill, 0)
        for s in range(sub):
            kt_ref[n_main, :, s * LANES:(s + 1) * LANES] = ke_ref[s * LANES:(s + 1) * LANES, :].T.astype(BF16)
        vb_ref[0:n_main * bk, :] = vm_ref[...].astype(BF16)
        vb_ref[n_main * bk:(n_main + 1) * bk, :] = ve_ref[...].astype(BF16)

    lane_q = _iota((bq, LANES), 1)
    q = q_ref[...] * (scale * math.log2(math.e))
    qm_ref[0] = jnp.where(lane_q < HEAD_DIM, q, 0.0).astype(BF16)
    qm_ref[1] = jnp.where(lane_q < HEAD_DIM, 0.0, q).astype(BF16)
    later = (_iota((bk, bk), 0) > _iota((bk, bk), 1)).astype(BF16)
    col = _iota((rc, bk), 1)
    row = _iota((rc, bk), 0)
    acc_ref[...] = jnp.zeros_like(acc_ref)
    c_ref[...] = jnp.zeros_like(c_ref)

    def block(j, mask_fn):
        kt = kt_ref[j]
        vblk = vb_ref[pl.ds(pl.multiple_of(j * bk, bk), bk), :]
        units = []
        for r0 in range(0, bq, rc):
            mask = None
            if mask_fn is not None:
                mask = mask_fn(r0)
                if mask is None:
                    continue
            units += [dict(r0=r0, hh=hh, mask=mask) for hh in range(2)]

        def logits(u):
            z = jnp.dot(qm_ref[u["hh"], u["r0"]:u["r0"] + rc, :], kt, preferred_element_type=F32)
            neg_abs = pltpu.bitcast(pltpu.bitcast(z, jnp.uint32) | jnp.uint32(0x80000000), F32)
            nlr = jnp.maximum(z, 0.0) + jnp.log2(1.0 + jnp.exp2(neg_abs))
            if u["mask"] is not None:
                nlr = jnp.where(u["mask"], nlr, 0.0)
            u.update(z=z, nlr=nlr, nlr16=nlr.astype(BF16), rowsum=jnp.sum(nlr, axis=1, keepdims=True))

        def newer(u):
            u["between"] = jnp.dot(u["nlr16"], later, preferred_element_type=F32)

        def weigh(u):
            rows = slice(u["r0"], u["r0"] + rc)
            cb = c_ref[u["hh"], rows, :]
            wgt = jnp.exp2(u["z"] - u["nlr"] - u["between"] - jnp.concatenate([cb] * sub, axis=1))
            if u["mask"] is not None:
                wgt = jnp.where(u["mask"], wgt, 0.0)
            acc_ref[u["hh"], rows, :] += jnp.dot(wgt.astype(BF16), vblk, preferred_element_type=F32)
            c_ref[u["hh"], rows, :] = cb + u["rowsum"]

        stages = (logits, newer, weigh)
        for step in range(len(units) + len(stages) - 1):
            for s_idx, stage in enumerate(stages):
                if 0 <= step - s_idx < len(units):
                    stage(units[step - s_idx])

    def passed():
        return jnp.min(jnp.minimum(c_ref[0], c_ref[1]))

    def sweep(j_first):
        def cond(carry):
            n, c_min = carry
            return jnp.logical_and(n < j_first, c_min < UNDERFLOW_LOG2)

        def body(carry):
            block(j_first - 1 - carry[0], None)
            return carry[0] + 1, passed()

        return lax.while_loop(cond, body, (jnp.int32(0), passed()))[1]

    if extra_newest:
        block(n_main, lambda r0: jnp.logical_and(col < row + r0, col < extra_valid))
        c_min = sweep(n_main)
    else:
        j_lo = (qi * bq) // bk
        for d in range(bq // bk - 1, -1, -1):
            block(j_lo + d, lambda r0, d=d: None if d * bk >= r0 + rc - 1 else d * bk + col < row + r0)
        c_min = sweep(j_lo)

        @pl.when(c_min < UNDERFLOW_LOG2)
        def _():
            block(n_main, lambda r0: col < extra_valid)
    o_ref[...] = jnp.where(lane_q < HEAD_DIM, acc_ref[0], acc_ref[1]).astype(o_ref.dtype)
    cmin_ref[...] = jnp.full(cmin_ref.shape, c_min, F32)


def _sb(q_arr, q_block, k_main, k_block, v_main, v_block, main_rows, k_extra, v_extra, extra_block, *, grid,
        out_rows, out_block, bq, bk, extra_valid, extra_newest, rc=256):
    w_b = k_extra.shape[1]
    n_main = main_rows // bk
    assert extra_newest or bq % bk == 0
    kern = functools.partial(_sb_kernel, bq=bq, bk=bk, rc=min(rc, bq), n_main=n_main, extra_valid=extra_valid,
                             extra_newest=extra_newest, scale=HEAD_DIM ** -0.5)
    return pl.pallas_call(
        kern,
        grid=grid,
        in_specs=[
            pl.BlockSpec((bq, LANES), q_block),
            pl.BlockSpec((main_rows, LANES), k_block),
            pl.BlockSpec((main_rows, LANES), v_block),
            pl.BlockSpec((bk, LANES), extra_block),
            pl.BlockSpec((bk, LANES), extra_block),
        ],
        out_specs=[pl.BlockSpec((bq, LANES), out_block),
                   pl.BlockSpec((1, 1, 1, 8, LANES), lambda b, p, i: (b, p, i, 0, 0))],
        out_shape=[jax.ShapeDtypeStruct((out_rows, w_b), BF16),
                   jax.ShapeDtypeStruct(tuple(grid) + (8, LANES), F32)],
        scratch_shapes=[
            pltpu.VMEM((n_main + 1, LANES, bk), BF16),
            pltpu.VMEM(((n_main + 1) * bk, LANES), BF16),
            pltpu.VMEM((2, bq, LANES), BF16),
            pltpu.VMEM((2, bq, LANES), F32),
            pltpu.VMEM((2, bq, LANES), F32),
        ],
        compiler_params=_params(("parallel", "parallel", "arbitrary")),
        name="stick_breaking_newest" if extra_newest else "stick_breaking",
    )(q_arr, k_main, v_main, k_extra, v_extra)


def _mix_kernel(*refs, n_gc, gw, w_a, alpha, n_exp, n_grp, tiles_p):
    xp_ref, xt_ref, oap_ref, oas_ref, obp_ref, obs_ref = refs[0:6]
    ga_refs = refs[6:6 + n_gc]
    gb_refs = refs[6 + n_gc:6 + 2 * n_gc]
    (wb_ref, wo_ref, lig_ref, lib_ref, lmg_ref, lmb_ref, wr_ref, br_ref,
     h_ref, comb_ref, mg_ref) = refs[6 + 2 * n_gc:]
    is_prompt = pl.program_id(0) < tiles_p
    x = jnp.where(is_prompt, xp_ref[...], xt_ref[...])
    oa = jnp.where(is_prompt, oap_ref[...], oas_ref[...])
    ob = jnp.where(is_prompt, obp_ref[...], obs_ref[...])
    for cidx in range(n_gc):
        cs = slice(cidx * gw, (cidx + 1) * gw)
        pa = jnp.dot(oa, wb_ref[0:w_a, cs], preferred_element_type=F32)
        pb = jnp.dot(ob, wb_ref[w_a:, cs], preferred_element_type=F32)
        mg_ref[:, cs] = (_sigmoid(ga_refs[cidx][...]) * pa + _sigmoid(gb_refs[cidx][...]) * pb).astype(BF16)
    mix = jnp.dot(mg_ref[...], wo_ref[...], preferred_element_type=F32)
    h = _layer_norm(x, lig_ref[...], lib_ref[...])
    h1 = _layer_norm(alpha * h + mix, lmg_ref[...], lmb_ref[...])
    h_ref[...] = h1

    lg = _dot3(h1, wr_ref[...]) + br_ref[...]
    lane = _iota(lg.shape, 1).astype(F32)
    neg, big = -1e30, 1e9
    epg = n_exp // n_grp
    is_g = jnp.logical_and(lane >= n_exp, lane < n_exp + n_grp)
    gmax = jnp.max(jnp.where(is_g, lg, neg), axis=1, keepdims=True)
    gsum = jnp.sum(jnp.where(is_g, jnp.exp(jnp.where(is_g, lg, neg) - gmax), 0.0), axis=1, keepdims=True)
    p_g = 1.0 / gsum
    g_idx = jnp.min(jnp.where(jnp.logical_and(is_g, lg == gmax), lane - n_exp, big), axis=1, keepdims=True)
    in_grp = jnp.logical_and(lane >= g_idx * epg, lane < (g_idx + 1.0) * epg)
    el = jnp.where(in_grp, lg, neg)
    m1 = jnp.max(el, axis=1, keepdims=True)
    i1 = jnp.min(jnp.where(jnp.logical_and(in_grp, lg == m1), lane, big), axis=1, keepdims=True)
    el2 = jnp.where(lane == i1, neg, el)
    m2 = jnp.max(el2, axis=1, keepdims=True)
    i2 = jnp.min(jnp.where(jnp.logical_and(el2 == m2, in_grp), lane, big), axis=1, keepdims=True)
    e2 = jnp.exp(m2 - m1)
    w1 = p_g / (1.0 + e2)
    w2 = p_g * e2 / (1.0 + e2)
    comb_ref[...] = (jnp.where(lane == 0.0, i1, 0.0) + jnp.where(lane == 1.0, i2, 0.0)
                     + jnp.where(lane == 2.0, w1, 0.0) + jnp.where(lane == 3.0, w2, 0.0))


def _mix(x_pt, o_a_ps, o_b_ps, proj, gate_off, wb, wo, ln_in, ln_mix, wr, br, *, rows, tm, alpha, n_exp, n_grp):
    d = x_pt[0].shape[1]
    w_a = o_a_ps[0].shape[1]
    tiles_p = x_pt[0].shape[0] // tm
    gw = math.gcd(gate_off, d)
    n_gc = d // gw
    gspec = lambda blk: pl.BlockSpec((tm, gw), lambda i, blk=blk: (i, blk))
    const = lambda a: pl.BlockSpec(a.shape, lambda i: (0,) * a.ndim, pipeline_mode=pl.Buffered(1))
    kern = functools.partial(_mix_kernel, n_gc=n_gc, gw=gw, w_a=w_a, alpha=alpha, n_exp=n_exp, n_grp=n_grp,
                             tiles_p=tiles_p)
    g0 = gate_off // gw
    args = (list(x_pt) + list(o_a_ps) + list(o_b_ps) + [proj] * (2 * n_gc)
            + [wb, wo, ln_in[0], ln_in[1], ln_mix[0], ln_mix[1], wr, br])
    in_specs = (_two_part_rows(tm, d, tiles_p) + _two_part_rows(tm, w_a, tiles_p)
                + _two_part_rows(tm, o_b_ps[0].shape[1], tiles_p)
                + [gspec(g0 + cidx) for cidx in range(n_gc)]
                + [gspec(g0 + n_gc + cidx) for cidx in range(n_gc)]
                + [const(a) for a in args[6 + 2 * n_gc:]])
    return pl.pallas_call(
        kern,
        grid=(rows // tm,),
        in_specs=in_specs,
        out_specs=[pl.BlockSpec((tm, d), lambda i: (i, 0)), pl.BlockSpec((tm, LANES), lambda i: (i, 0))],
        out_shape=[jax.ShapeDtypeStruct((rows, d), F32), jax.ShapeDtypeStruct((rows, LANES), F32)],
        scratch_shapes=[pltpu.VMEM((tm, d), BF16)],
        compiler_params=_params(("parallel",)),
        name="mix_out_router",
    )(*args)


def _gather_rows(src_hbm, idx_ref, base, n, dst, sem):
    def issue(r, carry):
        pltpu.make_async_copy(src_hbm.at[pl.ds(idx_ref[base + r], 1)], dst.at[pl.ds(r, 1)], sem).start()
        return carry

    lax.fori_loop(0, n, issue, 0, unroll=8)


def _wait_rows(src_hbm, n, dst, sem):
    pltpu.make_async_copy(src_hbm.at[pl.ds(0, n)], dst, sem).wait()


def _moe_ffn_kernel(te_ref, nu_ref, rt_ref, x_hbm, wg_ref, wu_ref, wd_ref, y_ref, xbuf, sem, wgb, wub, wdb, *, tm):
    t = pl.program_id(0)
    n_used = nu_ref[0]

    @pl.when(jnp.logical_and(t == 0, n_used > 0))
    def _():
        _gather_rows(x_hbm, rt_ref, 0, tm, xbuf.at[0], sem.at[0])

    @pl.when(t + 1 < n_used)
    def _():
        nxt = (t + 1) % 2
        _gather_rows(x_hbm, rt_ref, (t + 1) * tm, tm, xbuf.at[nxt], sem.at[nxt])

    @pl.when(t < n_used)
    def _():
        slot = t % 2
        _wait_rows(x_hbm, tm, xbuf.at[slot], sem.at[slot])

        @pl.when(jnp.logical_or(t == 0, te_ref[t] != te_ref[jnp.maximum(t - 1, 0)]))
        def _():
            wgb[...] = wg_ref[0].astype(BF16)
            wub[...] = wu_ref[0].astype(BF16)
            wdb[...] = wd_ref[0].astype(BF16)

        x = xbuf[slot].astype(BF16)
        hg = jnp.dot(x, wgb[...], preferred_element_type=F32)
        hu = jnp.dot(x, wub[...], preferred_element_type=F32)
        h = hg * _sigmoid(hg) * hu
        y_ref[...] = jnp.dot(h.astype(BF16), wdb[...], preferred_element_type=F32)

    @pl.when(t >= n_used)
    def _():
        y_ref[...] = jnp.zeros_like(y_ref)


def _moe_ffn(h1, tile_expert, n_used, row_token, wg, wu, wd, *, tm):
    d = h1.shape[1]
    n_exp, _, d_e = wg.shape
    n_tiles = tile_expert.shape[0]
    wspec = lambda shape: pl.BlockSpec((1,) + shape, lambda t, te, nu, rt: (te[t], 0, 0))
    return pl.pallas_call(
        functools.partial(_moe_ffn_kernel, tm=tm),
        grid_spec=pltpu.PrefetchScalarGridSpec(
            num_scalar_prefetch=3,
            grid=(n_tiles,),
            in_specs=[pl.BlockSpec(memory_space=pl.ANY), wspec((d, d_e)), wspec((d, d_e)), wspec((d_e, d))],
            out_specs=pl.BlockSpec((tm, d), lambda t, te, nu, rt: (t, 0)),
            scratch_shapes=[pltpu.VMEM((2, tm, d), F32), pltpu.SemaphoreType.DMA((2,)),
                            pltpu.VMEM((d, d_e), BF16), pltpu.VMEM((d, d_e), BF16), pltpu.VMEM((d_e, d), BF16)],
        ),
        out_shape=jax.ShapeDtypeStruct((n_tiles * tm, d), F32),
        compiler_params=pltpu.CompilerParams(dimension_semantics=("arbitrary",), vmem_limit_bytes=VMEM_LIMIT,
                                             disable_bounds_checks=True),
        name="moe_ffn",
    )(tile_expert, n_used, row_token, h1, wg, wu, wd)


def _moe_combine_kernel(pos_ref, y_hbm, x_ref, r_ref, g_ref, b_ref, op_ref, os_ref, buf, sem, *, tm, tiles_p, alpha):
    i = pl.program_id(0)
    n = pl.num_programs(0)

    def gather(tile, slot):
        def issue(r, carry):
            for k in range(TOP_K):
                p = pos_ref[TOP_K * (tile * tm + r) + k]
                pltpu.make_async_copy(y_hbm.at[pl.ds(p, 1)], buf.at[slot, pl.ds(k * tm + r, 1)], sem.at[slot]).start()
            return carry

        lax.fori_loop(0, tm, issue, 0, unroll=4)

    @pl.when(i == 0)
    def _():
        gather(0, 0)

    @pl.when(i + 1 < n)
    def _():
        gather(i + 1, (i + 1) % 2)

    slot = i % 2
    _wait_rows(y_hbm, TOP_K * tm, buf.at[slot], sem.at[slot])
    rec = r_ref[...]
    w1 = rec[:, 2:3]
    w2 = rec[:, 3:4]
    ffn = w1 * buf[slot, 0:tm, :] + w2 * buf[slot, tm:2 * tm, :]
    out = _layer_norm(alpha * x_ref[...] + ffn, g_ref[...], b_ref[...])

    @pl.when(i < tiles_p)
    def _():
        op_ref[...] = out

    @pl.when(i >= tiles_p)
    def _():
        os_ref[...] = out


def _moe_combine(pos, y_sorted, h1, route, ln_g, ln_b, *, n_p, tm, alpha):
    rows, d = h1.shape
    tiles_p = n_p // tm
    return pl.pallas_call(
        functools.partial(_moe_combine_kernel, tm=tm, tiles_p=tiles_p, alpha=alpha),
        grid_spec=pltpu.PrefetchScalarGridSpec(
            num_scalar_prefetch=1,
            grid=(rows // tm,),
            in_specs=[
                pl.BlockSpec(memory_space=pl.ANY),
                pl.BlockSpec((tm, d), lambda i, pos: (i, 0)),
                pl.BlockSpec((tm, LANES), lambda i, pos: (i, 0)),
                pl.BlockSpec((1, d), lambda i, pos: (0, 0)),
                pl.BlockSpec((1, d), lambda i, pos: (0, 0)),
            ],
            out_specs=[pl.BlockSpec((tm, d), lambda i, pos: (jnp.minimum(i, tiles_p - 1), 0)),
                       pl.BlockSpec((tm, d), lambda i, pos: (jnp.maximum(i - tiles_p, 0), 0))],
            scratch_shapes=[pltpu.VMEM((2, TOP_K * tm, d), F32), pltpu.SemaphoreType.DMA((2,))],
        ),
        out_shape=[jax.ShapeDtypeStruct((n_p, d), F32), jax.ShapeDtypeStruct((rows - n_p, d), F32)],
        compiler_params=pltpu.CompilerParams(dimension_semantics=("arbitrary",), vmem_limit_bytes=VMEM_LIMIT,
                                             disable_bounds_checks=True),
        name="moe_combine",
    )(pos, y_sorted, h1, route, ln_g, ln_b)


def _dispatch_plan(route, n_exp, tm):
    n_assign = route.shape[0] * TOP_K
    i32 = jnp.int32
    flat_e = route[:, :TOP_K].astype(i32).reshape(-1)
    order = jnp.argsort(flat_e, stable=True).astype(i32)
    rank = jnp.argsort(order).astype(i32)
    counts = jnp.sum((flat_e[:, None] == jnp.arange(n_exp, dtype=i32)[None, :]).astype(i32), axis=0)
    padded = (counts + tm - 1) // tm * tm
    ends_p = jnp.cumsum(padded)
    starts_p = ends_p - padded
    starts = jnp.cumsum(counts) - counts
    owner = lambda p: jnp.minimum(jnp.sum((p[:, None] >= ends_p[None, :]).astype(i32), axis=1), n_exp - 1)
    pos = rank - starts[flat_e] + starts_p[flat_e]
    n_tiles = (n_assign + n_exp * (tm - 1)) // tm + 1
    p = jnp.arange(n_tiles * tm, dtype=i32)
    e_p = owner(p)
    off = p - starts_p[e_p]
    src = jnp.clip(starts[e_p] + off, 0, n_assign - 1)
    row_token = jnp.where(off < counts[e_p], order[src] // TOP_K, 0)
    n_used = ends_p[-1] // tm
    tile_idx = jnp.minimum(jnp.arange(n_tiles, dtype=i32), n_used - 1)
    tile_expert = owner(tile_idx * tm)
    return tile_expert, n_used.reshape(1).astype(i32), row_token, pos


def _pad_cols(a, n):
    return jnp.pad(a, [(0, 0)] * (a.ndim - 1) + [(0, n - a.shape[-1])])


def kernel(x_prompt, x_sample, cache_sb_k, cache_sb_v, state_rwkv, state_rwkv_shift, meta_tokens, ln_in_g, ln_in_b, w_in, rwkv_mu, rwkv_w0, rwkv_w_up, rwkv_a0, rwkv_a_up, rwkv_g_up, rwkv_k_k, rwkv_k_a, rwkv_r_k, rwkv_lnx_g, rwkv_lnx_b, w_branch, w_out, ln_mix_g, ln_mix_b, router_group_w, router_group_b, router_expert_w, router_expert_b, moe_w_gate, moe_w_up, moe_w_down, ln_ffn_g, ln_ffn_b):
    depth = w_in.shape[0]
    assert depth == 1, "single-layer step only"
    nb, seq, d = x_prompt.shape
    db, ds = x_sample.shape[:2]
    n_meta = meta_tokens.shape[0]
    past = cache_sb_k.shape[2]
    w_a = rwkv_w0.shape[1]
    r_d, r_a, r_g = rwkv_w_up.shape[1], rwkv_a_up.shape[1], rwkv_g_up.shape[1]
    w_shift = 3 * w_a + r_d + r_a + r_g
    w_b = (w_in.shape[2] - w_shift - 2 * d) // 3
    h_a, h_b = w_a // HEAD_DIM, w_b // HEAD_DIM
    n_grp = router_group_w.shape[2]
    n_exp = router_expert_w.shape[2]
    alpha = (2 * depth) ** 0.25
    assert w_a % LANES == 0 and w_b % LANES == 0 and seq % 256 == 0 and ds == CHUNK and n_meta <= CHUNK
    assert n_exp + n_grp <= LANES

    dw, da, dg = (_round_up(n, LANES) for n in (r_d, r_a, r_g))
    pr_cols = 3 * w_a + dw + da + dg
    wi = w_in[0]
    s0 = 3 * w_a
    segs = [wi[:, :s0], _pad_cols(wi[:, s0:s0 + r_d], dw), _pad_cols(wi[:, s0 + r_d:s0 + r_d + r_a], da),
            _pad_cols(wi[:, s0 + r_d + r_a:w_shift], dg), wi[:, w_shift:]]
    w_pad = jnp.concatenate(segs, axis=1).astype(BF16)
    p_cols = w_pad.shape[1]
    q_off, k_off, v_off, gate_off = pr_cols, pr_cols + w_b, pr_cols + 2 * w_b, pr_cols + 3 * w_b

    def to_padded_rwkv_cols(a):
        return jnp.concatenate([a[..., :s0], _pad_cols(a[..., s0:s0 + r_d], dw),
                                _pad_cols(a[..., s0 + r_d:s0 + r_d + r_a], da),
                                _pad_cols(a[..., s0 + r_d + r_a:], dg)], axis=-1)

    def from_padded_rwkv_cols(a):
        return jnp.concatenate([a[..., :s0], a[..., s0:s0 + r_d], a[..., s0 + dw:s0 + dw + r_a],
                                a[..., s0 + dw + da:s0 + dw + da + r_g]], axis=-1)

    n_p, n_s = nb * seq, db * ds
    rows_main = n_p + n_s
    tm1 = _pick_tile(math.gcd(n_p, n_s), (512, 256, 128))
    rows_all = _round_up(rows_main + n_meta, tm1)
    x_p = x_prompt.reshape(n_p, d)
    x_t = jnp.concatenate([x_sample.reshape(n_s, d), meta_tokens.astype(F32),
                           jnp.zeros((rows_all - rows_main - n_meta, d), F32)], axis=0)
    tn1 = _pick_tile(p_cols, (1792, 512, 256, 128))
    proj = _inproj(x_p, x_t, ln_in_g[None], ln_in_b[None], w_pad, tm1, tn1)
    proj_meta = proj[rows_main:rows_main + n_meta]

    row2 = lambda a: a.reshape(1, -1)
    rw = dict(mu=to_padded_rwkv_cols(rwkv_mu[0])[None], w0=row2(rwkv_w0[0]), a0=row2(rwkv_a0[0]),
              k_k=row2(rwkv_k_k[0]), k_a=row2(rwkv_k_a[0]), r_k=row2(rwkv_r_k[0]),
              lnx_g=row2(rwkv_lnx_g[0]), lnx_b=row2(rwkv_lnx_b[0]),
              w_up=jnp.pad(rwkv_w_up[0], ((0, dw - r_d), (0, 0))),
              a_up=jnp.pad(rwkv_a_up[0], ((0, da - r_a), (0, 0))),
              g_up=jnp.pad(rwkv_g_up[0], ((0, dg - r_g), (0, 0))))
    n_pairs_a = w_a // LANES
    n_null = CHUNK - n_meta
    head = jnp.concatenate([jnp.zeros((n_null, pr_cols), F32), proj_meta[:, :pr_cols]], axis=0)
    tiles_p = seq // CHUNK
    o_a_p, hbd_p = _rwkv(proj, head, jnp.zeros((nb, 1, pr_cols), F32),
                         jnp.zeros((nb, n_pairs_a, LANES, LANES), F32), rw,
                         n_seq=nb, n_tiles=tiles_p + 1,
                         row_block=lambda s, t: s * tiles_p + jnp.maximum(t - 1, 0),
                         out_block=lambda s, t: s * tiles_p + jnp.maximum(t - 1, 0), out_rows=n_p, has_head=True, n_null=n_null, pr_cols=pr_cols, widths=(dw, da, dg))

    def to_hbd(state):
        n = state.shape[0]
        hkv = jnp.swapaxes(state, -1, -2).reshape(n, n_pairs_a, 2, HEAD_DIM, HEAD_DIM)
        z = jnp.zeros_like(hkv[:, :, 0])
        top = jnp.concatenate([hkv[:, :, 0], z], axis=-1)
        bot = jnp.concatenate([z, hkv[:, :, 1]], axis=-1)
        return jnp.concatenate([top, bot], axis=-2)

    def from_hbd(hbd):
        h0 = hbd[:, :, :HEAD_DIM, :HEAD_DIM]
        h1 = hbd[:, :, HEAD_DIM:, HEAD_DIM:]
        hkv = jnp.stack([h0, h1], axis=2).reshape(hbd.shape[0], h_a, HEAD_DIM, HEAD_DIM)
        return jnp.swapaxes(hkv, -1, -2)

    base_s = n_p // CHUNK
    o_a_s, hbd_s = _rwkv(proj, head, to_padded_rwkv_cols(state_rwkv_shift[0]), to_hbd(state_rwkv[0]), rw,
                         n_seq=db, n_tiles=ds // CHUNK,
                         row_block=lambda s, t: base_s + s * (ds // CHUNK) + t,
                         out_block=lambda s, t: s * (ds // CHUNK) + t, out_rows=n_s, has_head=False, n_null=0, pr_cols=pr_cols, widths=(dw, da, dg))

    bk = 256
    bq_p = _pick_tile(seq, (512, 256))
    assert past % bk == 0 and n_meta <= bk and ds <= bk
    n_pairs_b = w_b // LANES
    qc, kc, vc = q_off // LANES, k_off // LANES, v_off // LANES
    k_meta, v_meta = proj_meta[:, k_off:k_off + w_b], proj_meta[:, v_off:v_off + w_b]
    pad_rows = lambda a: jnp.pad(a, ((0, bk - a.shape[0]), (0, 0)))
    qt_p = seq // bq_p
    o_b_p, _ = _sb(proj, lambda b, p, i: (b * qt_p + i, qc + p),
                   proj, lambda b, p, i: (b, kc + p), proj, lambda b, p, i: (b, vc + p), seq,
                   pad_rows(k_meta), pad_rows(v_meta), lambda b, p, i: (0, p),
                   grid=(nb, n_pairs_b, qt_p), out_rows=n_p, out_block=lambda b, p, i: (b * qt_p + i, p),
                   bq=bq_p, bk=bk, extra_valid=n_meta, extra_newest=False)

    new = lambda off: proj[n_p:rows_main, off:off + w_b].reshape(db, ds, w_b)
    k_new, v_new = new(k_off), new(v_off)
    pad_seq = lambda a: jnp.pad(a, ((0, 0), (0, bk - ds), (0, 0))).reshape(db * bk, w_b)

    def sb_sample(first_key):
        n_keys = past - first_key
        rows = lambda c: c[0][:, first_key:].reshape(db * n_keys, w_b)
        return _sb(proj, lambda b, p, i: (n_p // ds + b, qc + p),
                   rows(cache_sb_k), lambda b, p, i: (b, p), rows(cache_sb_v), lambda b, p, i: (b, p), n_keys,
                   pad_seq(k_new), pad_seq(v_new), lambda b, p, i: (b, p),
                   grid=(db, n_pairs_b, 1), out_rows=n_s, out_block=lambda b, p, i: (b, p),
                   bq=ds, bk=bk, extra_valid=ds, extra_newest=True)

    recent = min(past, 2 * bk)
    o_b_s, c_min = sb_sample(past - recent)
    if recent < past:
        o_b_s = lax.cond(jnp.min(c_min) >= UNDERFLOW_LOG2, lambda: o_b_s, lambda: sb_sample(0)[0])

    wr = jnp.concatenate([router_expert_w[0], router_group_w[0]], axis=1)
    wr = _pad_cols(wr, LANES)
    br = _pad_cols(jnp.concatenate([router_expert_b[0], router_group_b[0]])[None], LANES)
    tm4 = _pick_tile(math.gcd(n_p, n_s), (256, 128))
    h1, route = _mix((x_p, x_t), (o_a_p, o_a_s), (o_b_p, o_b_s), proj, gate_off,
                     w_branch[0].astype(BF16), w_out[0].astype(BF16),
                     (ln_in_g[None], ln_in_b[None]), (ln_mix_g, ln_mix_b), wr, br,
                     rows=rows_main, tm=tm4, alpha=alpha, n_exp=n_exp, n_grp=n_grp)

    tm5 = _pick_tile(n_s, (256, 128))
    assert n_p % tm5 == 0
    tile_expert, n_used, row_token, pos = _dispatch_plan(route, n_exp, tm5)
    y_sorted = _moe_ffn(h1, tile_expert, n_used, row_token, moe_w_gate[0], moe_w_up[0], moe_w_down[0], tm=tm5)
    y_p, y_s = _moe_combine(pos, y_sorted, h1, route, ln_ffn_g, ln_ffn_b, n_p=n_p, tm=tm5, alpha=alpha)

    y_p = y_p.reshape(nb, seq, d)
    y_s = y_s.reshape(db, ds, d)
    heads = lambda a, n, t: a.reshape(1, n, t, h_b, HEAD_DIM)
    with_meta = lambda m, off: jnp.concatenate(
        [jnp.broadcast_to(m[None], (nb, n_meta, w_b)), proj[:n_p, off:off + w_b].reshape(nb, seq, w_b)], axis=1)
    sb_k_p = heads(with_meta(k_meta, k_off), nb, n_meta + seq)
    sb_v_p = heads(with_meta(v_meta, v_off), nb, n_meta + seq)
    last_p = proj[seq - 1:n_p:seq, :pr_cols]
    last_s = proj[n_p + ds - 1:rows_main:ds, :pr_cols]
    shift_p = from_padded_rwkv_cols(last_p)[None, :, None, :]
    shift_s = from_padded_rwkv_cols(last_s)[None, :, None, :]
    return (y_p, y_s, sb_k_p, sb_v_p, from_hbd(hbd_p)[None], shift_p,
            heads(k_new, db, ds), heads(v_new, db, ds), from_hbd(hbd_s)[None], shift_s)
```

```python
import functools
import math

import jax
import jax.numpy as jnp
from jax import lax
from jax.experimental import pallas as pl
from jax.experimental.pallas import tpu as pltpu

F32 = jnp.float32
BF16 = jnp.bfloat16

HEAD_DIM = 64
LANES = 128
CHUNK = 64
TOP_K = 2
LN_EPS = 1e-5
GN_EPS = 64e-5
UNDERFLOW_LOG2 = 150.0
VMEM_LIMIT = 56 * 1024 * 1024


def _round_up(n, m):
    return -(-n // m) * m


def _pick_tile(n, candidates):
    for c in candidates:
        if n % c == 0:
            return c
    raise ValueError(f"no tile in {candidates} divides {n}")


def _params(sem, vmem=VMEM_LIMIT):
    return pltpu.CompilerParams(dimension_semantics=sem, vmem_limit_bytes=vmem)


def _dot(a, b):
    return jnp.dot(a.astype(BF16), b.astype(BF16), preferred_element_type=F32)


def _split2(x):
    hi = x.astype(BF16)
    lo = (x - hi.astype(F32)).astype(BF16)
    return hi, lo


def _split3(x):
    hi = x.astype(BF16)
    r1 = x - hi.astype(F32)
    mid = r1.astype(BF16)
    lo = (r1 - mid.astype(F32)).astype(BF16)
    return hi, mid, lo


def _dot3(a, b):
    ah, al = _split2(a)
    bh, bl = _split2(b)
    d = lambda x, y: jnp.dot(x, y, preferred_element_type=F32)
    return d(ah, bh) + (d(ah, bl) + d(al, bh))


def _dot3_nt(a, b):
    ah, al = _split2(a)
    bh, bl = _split2(b)
    d = lambda x, y: lax.dot_general(x, y, (((1,), (1,)), ((), ())), preferred_element_type=F32)
    return d(ah, bh) + (d(ah, bl) + d(al, bh))


def _dot_exact_rhs(a, b_bf16):
    hi, lo = _split2(a)
    d = lambda x: jnp.dot(x, b_bf16, preferred_element_type=F32)
    return d(hi) + d(lo)


def _dot_exact_lhs(a_bf16, b):
    hi, mid, lo = _split3(b)
    d = lambda x: jnp.dot(a_bf16, x, preferred_element_type=F32)
    return d(hi) + (d(mid) + d(lo))


def _layer_norm(x, g, b):
    mu = jnp.mean(x, axis=-1, keepdims=True)
    xc = x - mu
    var = jnp.mean(xc * xc, axis=-1, keepdims=True)
    return xc * lax.rsqrt(var + LN_EPS) * g + b


def _sigmoid(x):
    return 1.0 / (1.0 + jnp.exp(-x))


def _softplus(x):
    return jnp.maximum(x, 0.0) + jnp.log(1.0 + jnp.exp(-jnp.abs(x)))


def _iota(shape, dim):
    return lax.broadcasted_iota(jnp.int32, shape, dim)


def _inproj_kernel(xp_ref, xt_ref, g_ref, b_ref, w_ref, o_ref, xn_ref, *, tiles_p):
    @pl.when(pl.program_id(1) == 0)
    def _():
        x = jnp.where(pl.program_id(0) < tiles_p, xp_ref[...], xt_ref[...])
        xn_ref[...] = _layer_norm(x, g_ref[...], b_ref[...]).astype(BF16)

    o_ref[...] = jnp.dot(xn_ref[...], w_ref[...], preferred_element_type=F32)


def _two_part_rows(tm, d, tiles_p):
    return [pl.BlockSpec((tm, d), lambda i, *_: (jnp.minimum(i, tiles_p - 1), 0)),
            pl.BlockSpec((tm, d), lambda i, *_: (jnp.maximum(i - tiles_p, 0), 0))]


def _inproj(x_p, x_t, ln_g, ln_b, w_pad, tm, tn):
    d = x_p.shape[1]
    rows = x_p.shape[0] + x_t.shape[0]
    tiles_p = x_p.shape[0] // tm
    cols = w_pad.shape[1]
    return pl.pallas_call(
        functools.partial(_inproj_kernel, tiles_p=tiles_p),
        grid=(rows // tm, cols // tn),
        in_specs=_two_part_rows(tm, d, tiles_p) + [
            pl.BlockSpec((1, d), lambda i, j: (0, 0)),
            pl.BlockSpec((1, d), lambda i, j: (0, 0)),
            pl.BlockSpec((d, tn), lambda i, j: (0, j)),
        ],
        out_specs=pl.BlockSpec((tm, tn), lambda i, j: (i, j)),
        out_shape=jax.ShapeDtypeStruct((rows, cols), F32),
        scratch_shapes=[pltpu.VMEM((tm, d), BF16)],
        compiler_params=_params(("parallel", "arbitrary")),
        name="inproj",
    )(x_p, x_t, ln_g, ln_b, w_pad)


def _rwkv_kernel(p_ref, head_ref, prev_ref, h0_ref, mu_ref, w0_ref, a0_ref, kk_ref, ka_ref, rk_ref,
                 lng_ref, lnb_ref, wup_ref, aup_ref, gup_ref,
                 o_ref, hout_ref, carry_ref, hbd_ref, *, n_pairs, has_head, n_null, dw, da, dg):
    t = pl.program_id(1)
    nt = pl.num_programs(1)
    w_a = n_pairs * LANES
    c = CHUNK

    @pl.when(t == 0)
    def _():
        carry_ref[...] = prev_ref[0]
        hbd_ref[...] = h0_ref[0]

    if has_head:
        p = jnp.where(t == 0, head_ref[...], p_ref[...])
    else:
        p = p_ref[...]
    row = _iota((c, 1), 0)
    shifted = jnp.where(row == 0, carry_ref[...], pltpu.roll(p, 1, axis=0))
    carry_ref[...] = p[c - 1:c, :]
    xm = p + (shifted - p) * mu_ref[...]

    r = xm[:, 0:w_a]
    k = xm[:, w_a:2 * w_a]
    v = xm[:, 2 * w_a:3 * w_a]
    o = 3 * w_a
    xw = xm[:, o:o + dw]
    xa = xm[:, o + dw:o + dw + da]
    xg = xm[:, o + dw + da:o + dw + da + dg]

    w = -_softplus(-(w0_ref[...] + _dot3(jnp.tanh(xw), wup_ref[...]))) - 0.5
    lw = -jnp.exp(w)
    a = _sigmoid(a0_ref[...] + _dot3(xa, aup_ref[...]))
    g = _dot3(_sigmoid(xg), gup_ref[...])
    kk0 = k * kk_ref[...]
    kmod = k * (1.0 + (a - 1.0) * ka_ref[...])
    if has_head:
        valid = jnp.logical_or(row >= n_null, t > 0)
        lw = jnp.where(valid, lw, 0.0)
        kk0 = jnp.where(valid, kk0, 0.0)
        kmod = jnp.where(valid, kmod, 0.0)

    tri = (_iota((c, c), 1) <= _iota((c, c), 0)).astype(BF16)
    lcum = _dot_exact_lhs(tri, lw)
    lend = lcum[c - 1:c, :]
    e_pos = jnp.exp(lcum)
    e_prev = jnp.exp(lcum - lw)
    e_neg = jnp.exp(-lcum)
    e_end = jnp.exp(lend - lcum)

    lane = _iota((c, LANES), 1)
    rowc = _iota((c, LANES), 0)
    first = lane < HEAD_DIM
    col = jnp.where(first, lane, lane - HEAD_DIM)
    strict = col < rowc
    incl = col <= rowc
    eye = (col == rowc).astype(F32)
    rr = _iota((LANES, LANES), 0)
    cc = _iota((LANES, LANES), 1)
    bd_mask = (rr < HEAD_DIM) == (cc < HEAD_DIM)
    ones_bd = bd_mask.astype(BF16)
    avg_bd = (bd_mask.astype(F32) * (1.0 / HEAD_DIM)).astype(BF16)
    ones_cl = jnp.ones((c, LANES), BF16)

    def bd(x):
        return jnp.concatenate([jnp.where(first, x, 0.0), jnp.where(first, 0.0, x)], axis=0)

    def pack(x1, x2, left):
        if left:
            return jnp.where(first, x1, pltpu.roll(x2, HEAD_DIM, axis=1))
        return jnp.where(first, pltpu.roll(x1, HEAD_DIM, axis=1), x2)

    prs = range(n_pairs)
    sls = [slice(pr * LANES, (pr + 1) * LANES) for pr in prs]
    hs = [hbd_ref[pr] for pr in prs]
    kk0p = [kk0[:, sl] for sl in sls]
    ss = [_dot_exact_rhs(x * x, ones_bd) for x in kk0p]
    kkn = [x / jnp.maximum(jnp.sqrt(s_), 1e-12) for x, s_ in zip(kk0p, ss)]
    bb = [kkn[pr] * a[:, sls[pr]] for pr in prs]
    a_t = [-kkn[pr] * e_prev[:, sls[pr]] for pr in prs]
    b_t = [bb[pr] * e_neg[:, sls[pr]] for pr in prs]
    k_t = [kmod[:, sls[pr]] * e_neg[:, sls[pr]] for pr in prs]
    r_t = [r[:, sls[pr]] * e_pos[:, sls[pr]] for pr in prs]
    b_h = [bb[pr] * e_end[:, sls[pr]] for pr in prs]
    k_h = [kmod[:, sls[pr]] * e_end[:, sls[pr]] for pr in prs]
    vp = [v[:, sl] for sl in sls]
    vbd = [bd(x) for x in vp]

    lhs = [jnp.concatenate([jnp.where(first, a_t[pr], 0.0), jnp.where(first, 0.0, a_t[pr]),
                            jnp.where(first, r_t[pr], 0.0), jnp.where(first, 0.0, r_t[pr])], axis=0) for pr in prs]
    rhs = [jnp.concatenate([b_t[pr], k_t[pr]], axis=0) for pr in prs]
    sc = [_dot3_nt(lhs[pr], rhs[pr]) for pr in prs]
    m_ab = [jnp.where(strict, pack(s_[0:c], s_[c:2 * c], True), 0.0) for s_ in sc]
    m_ak = [jnp.where(strict, pack(s_[0:c], s_[c:2 * c], False), 0.0) for s_ in sc]
    m_rb = [jnp.where(incl, pack(s_[2 * c:3 * c], s_[3 * c:4 * c], True), 0.0) for s_ in sc]
    m_rk = [jnp.where(incl, pack(s_[2 * c:3 * c], s_[3 * c:4 * c], False), 0.0) for s_ in sc]

    same = lambda m: (col >> (m.bit_length() - 1)) == (rowc >> (m.bit_length() - 1))
    tinv = [eye + jnp.where(same(2), x, 0.0) for x in m_ab]
    m = 4
    while m <= c:
        off_mask = jnp.logical_and(same(m), jnp.logical_not(same(m // 2)))
        a_off = [jnp.where(off_mask, x, 0.0) for x in m_ab]
        left = [_dot(tinv[pr], bd(a_off[pr])) for pr in prs]
        tinv = [tinv[pr] + _dot(left[pr], bd(tinv[pr])) for pr in prs]
        m *= 2

    ah = [_dot3(a_t[pr], hs[pr]) for pr in prs]
    akv = [_dot(m_ak[pr], vbd[pr]) for pr in prs]
    rh = [_dot(r_t[pr], hs[pr]) for pr in prs]
    rkv = [_dot(m_rk[pr], vbd[pr]) for pr in prs]
    u = [_dot3(tinv[pr], bd(ah[pr] + akv[pr])) for pr in prs]
    y = [rh[pr] + rkv[pr] + _dot(m_rb[pr], bd(u[pr])) for pr in prs]

    lend_col = [_dot_exact_rhs(lw[:, sl].T, ones_cl) for sl in sls]
    upd = [_dot(jnp.concatenate([b_h[pr], k_h[pr]], axis=0).T, jnp.concatenate([u[pr], vp[pr]], axis=0))
           for pr in prs]
    h_new = [jnp.where(bd_mask, hs[pr] * jnp.exp(lend_col[pr]) + upd[pr], 0.0) for pr in prs]

    mean = [_dot_exact_rhs(x, avg_bd) for x in y]
    yc = [y[pr] - mean[pr] for pr in prs]
    var = [_dot_exact_rhs(x * x, avg_bd) for x in yc]
    bonus = [_dot_exact_rhs(r[:, sls[pr]] * kmod[:, sls[pr]] * rk_ref[:, sls[pr]], ones_bd) * vp[pr] for pr in prs]
    for pr in prs:
        sl = sls[pr]
        yn = yc[pr] * lax.rsqrt(var[pr] + GN_EPS) * lng_ref[:, sl] + lnb_ref[:, sl]
        o_ref[:, sl] = ((yn + bonus[pr]) * g[:, sl]).astype(o_ref.dtype)
    for pr in prs:
        hbd_ref[pr] = h_new[pr]

    @pl.when(t == nt - 1)
    def _():
        hout_ref[0] = hbd_ref[...]


def _rwkv(proj, head, prev, h0, rw, *, n_seq, n_tiles, row_block, out_block, out_rows, has_head, n_null, pr_cols,
          widths):
    n_pairs = h0.shape[1]
    w_a = n_pairs * LANES
    dw, da, dg = widths
    vec = lambda n: pl.BlockSpec((1, n), lambda s, t: (0, 0))
    mat = lambda a: pl.BlockSpec(a.shape, lambda s, t: (0, 0))
    kern = functools.partial(_rwkv_kernel, n_pairs=n_pairs, has_head=has_head, n_null=n_null,
                             dw=dw, da=da, dg=dg)
    return pl.pallas_call(
        kern,
        grid=(n_seq, n_tiles),
        in_specs=[
            pl.BlockSpec((CHUNK, pr_cols), lambda s, t: (row_block(s, t), 0)),
            pl.BlockSpec((CHUNK, pr_cols), lambda s, t: (0, 0)),
            pl.BlockSpec((1, 1, pr_cols), lambda s, t: (s, 0, 0)),
            pl.BlockSpec((1, n_pairs, LANES, LANES), lambda s, t: (s, 0, 0, 0)),
            vec(pr_cols), vec(w_a), vec(w_a), vec(w_a), vec(w_a), vec(w_a), vec(w_a), vec(w_a),
            mat(rw["w_up"]), mat(rw["a_up"]), mat(rw["g_up"]),
        ],
        out_specs=[
            pl.BlockSpec((CHUNK, w_a), lambda s, t: (out_block(s, t), 0)),
            pl.BlockSpec((1, n_pairs, LANES, LANES), lambda s, t: (s, 0, 0, 0)),
        ],
        out_shape=[
            jax.ShapeDtypeStruct((out_rows, w_a), BF16),
            jax.ShapeDtypeStruct((n_seq, n_pairs, LANES, LANES), F32),
        ],
        scratch_shapes=[pltpu.VMEM((1, pr_cols), F32), pltpu.VMEM((n_pairs, LANES, LANES), F32)],
        compiler_params=_params(("parallel", "arbitrary")),
        name="rwkv_head" if has_head else "rwkv",
    )(proj, head, prev, h0, rw["mu"], rw["w0"], rw["a0"], rw["k_k"], rw["k_a"], rw["r_k"],
      rw["lnx_g"], rw["lnx_b"], rw["w_up"], rw["a_up"], rw["g_up"])


def _sb_kernel(q_ref, km_ref, vm_ref, ke_ref, ve_ref, o_ref, cmin_ref, kt_ref, vb_ref, qm_ref, acc_ref, c_ref, *,
               bq, bk, rc, n_main, extra_valid, extra_newest, scale):
    qi = pl.program_id(2)
    sub = bk // LANES

    @pl.when(qi == 0)
    def _():
        def fill(j, carry):
            for s in range(sub):
                blk = km_ref[pl.ds(pl.multiple_of(j * bk + s * LANES, LANES), LANES), :]
                kt_ref[j, :, s * LANES:(s + 1) * LANES] = blk.T.astype(BF16)
            return carry

        lax.fori_loop(0, n_main, fill, 0)
        for s in range(sub):
            kt_ref[n_main, :, s * LANES:(s + 1) * LANES] = ke_ref[s * LANES:(s + 1) * LANES, :].T.astype(BF16)
        vb_ref[0:n_main * bk, :] = vm_ref[...].astype(BF16)
        vb_ref[n_main * bk:(n_main + 1) * bk, :] = ve_ref[...].astype(BF16)

    lane_q = _iota((bq, LANES), 1)
    q = q_ref[...] * (scale * math.log2(math.e))
    qm_ref[0] = jnp.where(lane_q < HEAD_DIM, q, 0.0).astype(BF16)
    qm_ref[1] = jnp.where(lane_q < HEAD_DIM, 0.0, q).astype(BF16)
    later = (_iota((bk, bk), 0) > _iota((bk, bk), 1)).astype(BF16)
    col = _iota((rc, bk), 1)
    row = _iota((rc, bk), 0)
    acc_ref[...] = jnp.zeros_like(acc_ref)
    c_ref[...] = jnp.zeros_like(c_ref)

    def block(j, mask_fn):
        kt = kt_ref[j]
        vblk = vb_ref[pl.ds(pl.multiple_of(j * bk, bk), bk), :]
        units = []
        for r0 in range(0, bq, rc):
            mask = None
            if mask_fn is not None:
                mask = mask_fn(r0)
                if mask is None:
                    continue
            units += [dict(r0=r0, hh=hh, mask=mask) for hh in range(2)]

        def logits(u):
            z = jnp.dot(qm_ref[u["hh"], u["r0"]:u["r0"] + rc, :], kt, preferred_element_type=F32)
            neg_abs = pltpu.bitcast(pltpu.bitcast(z, jnp.uint32) | jnp.uint32(0x80000000), F32)
            nlr = jnp.maximum(z, 0.0) + jnp.log2(1.0 + jnp.exp2(neg_abs))
            if u["mask"] is not None:
                nlr = jnp.where(u["mask"], nlr, 0.0)
            u.update(z=z, nlr=nlr, nlr16=nlr.astype(BF16), rowsum=jnp.sum(nlr, axis=1, keepdims=True))

        def newer(u):
            u["between"] = jnp.dot(u["nlr16"], later, preferred_element_type=F32)

        def weigh(u):
            rows = slice(u["r0"], u["r0"] + rc)
            cb = c_ref[u["hh"], rows, :]
            wgt = jnp.exp2(u["z"] - u["nlr"] - u["between"] - jnp.concatenate([cb] * sub, axis=1))
            if u["mask"] is not None:
                wgt = jnp.where(u["mask"], wgt, 0.0)
            acc_ref[u["hh"], rows, :] += jnp.dot(wgt.astype(BF16), vblk, preferred_element_type=F32)
            c_ref[u["hh"], rows, :] = cb + u["rowsum"]

        stages = (logits, newer, weigh)
        for step in range(len(units) + len(stages) - 1):
            for s_idx, stage in enumerate(stages):
                if 0 <= step - s_idx < len(units):
                    stage(units[step - s_idx])

    def passed():
        return jnp.min(jnp.minimum(c_ref[0], c_ref[1]))

    def sweep(j_first):
        def cond(carry):
            n, c_min = carry
            return jnp.logical_and(n < j_first, c_min < UNDERFLOW_LOG2)

        def body(carry):
            block(j_first - 1 - carry[0], None)
            return carry[0] + 1, passed()

        return lax.while_loop(cond, body, (jnp.int32(0), passed()))[1]

    if extra_newest:
        block(n_main, lambda r0: jnp.logical_and(col < row + r0, col < extra_valid))
        c_min = sweep(n_main)
    else:
        j_lo = (qi * bq) // bk
        for d in range(bq // bk - 1, -1, -1):
            block(j_lo + d, lambda r0, d=d: None if d * bk >= r0 + rc - 1 else d * bk + col < row + r0)
        c_min = sweep(j_lo)

        @pl.when(c_min < UNDERFLOW_LOG2)
        def _():
            block(n_main, lambda r0: col < extra_valid)
    o_ref[...] = jnp.where(lane_q < HEAD_DIM, acc_ref[0], acc_ref[1]).astype(o_ref.dtype)
    cmin_ref[...] = jnp.full(cmin_ref.shape, c_min, F32)


def _sb(q_arr, q_block, k_main, k_block, v_main, v_block, main_rows, k_extra, v_extra, extra_block, *, grid,
        out_rows, out_block, bq, bk, extra_valid, extra_newest, rc=256):
    w_b = k_extra.shape[1]
    n_main = main_rows // bk
    assert extra_newest or bq % bk == 0
    kern = functools.partial(_sb_kernel, bq=bq, bk=bk, rc=min(rc, bq), n_main=n_main, extra_valid=extra_valid,
                             extra_newest=extra_newest, scale=HEAD_DIM ** -0.5)
    return pl.pallas_call(
        kern,
        grid=grid,
        in_specs=[
            pl.BlockSpec((bq, LANES), q_block),
            pl.BlockSpec((main_rows, LANES), k_block),
            pl.BlockSpec((main_rows, LANES), v_block),
            pl.BlockSpec((bk, LANES), extra_block),
            pl.BlockSpec((bk, LANES), extra_block),
        ],
        out_specs=[pl.BlockSpec((bq, LANES), out_block),
                   pl.BlockSpec((1, 1, 1, 8, LANES), lambda b, p, i: (b, p, i, 0, 0))],
        out_shape=[jax.ShapeDtypeStruct((out_rows, w_b), BF16),
                   jax.ShapeDtypeStruct(tuple(grid) + (8, LANES), F32)],
        scratch_shapes=[
            pltpu.VMEM((n_main + 1, LANES, bk), BF16),
            pltpu.VMEM(((n_main + 1) * bk, LANES), BF16),
            pltpu.VMEM((2, bq, LANES), BF16),
            pltpu.VMEM((2, bq, LANES), F32),
            pltpu.VMEM((2, bq, LANES), F32),
        ],
        compiler_params=_params(("parallel", "parallel", "arbitrary")),
        name="stick_breaking_newest" if extra_newest else "stick_breaking",
    )(q_arr, k_main, v_main, k_extra, v_extra)


def _mix_kernel(*refs, n_gc, gw, w_a, alpha, n_exp, n_grp, tiles_p):
    xp_ref, xt_ref, oap_ref, oas_ref, obp_ref, obs_ref = refs[0:6]
    ga_refs = refs[6:6 + n_gc]
    gb_refs = refs[6 + n_gc:6 + 2 * n_gc]
    (wb_ref, wo_ref, lig_ref, lib_ref, lmg_ref, lmb_ref, wr_ref, br_ref,
     h_ref, comb_ref, mg_ref) = refs[6 + 2 * n_gc:]
    is_prompt = pl.program_id(0) < tiles_p
    x = jnp.where(is_prompt, xp_ref[...], xt_ref[...])
    oa = jnp.where(is_prompt, oap_ref[...], oas_ref[...])
    ob = jnp.where(is_prompt, obp_ref[...], obs_ref[...])
    for cidx in range(n_gc):
        cs = slice(cidx * gw, (cidx + 1) * gw)
        pa = jnp.dot(oa, wb_ref[0:w_a, cs], preferred_element_type=F32)
        pb = jnp.dot(ob, wb_ref[w_a:, cs], preferred_element_type=F32)
        mg_ref[:, cs] = (_sigmoid(ga_refs[cidx][...]) * pa + _sigmoid(gb_refs[cidx][...]) * pb).astype(BF16)
    mix = jnp.dot(mg_ref[...], wo_ref[...], preferred_element_type=F32)
    h = _layer_norm(x, lig_ref[...], lib_ref[...])
    h1 = _layer_norm(alpha * h + mix, lmg_ref[...], lmb_ref[...])
    h_ref[...] = h1

    lg = _dot3(h1, wr_ref[...]) + br_ref[...]
    lane = _iota(lg.shape, 1).astype(F32)
    neg, big = -1e30, 1e9
    epg = n_exp // n_grp
    is_g = jnp.logical_and(lane >= n_exp, lane < n_exp + n_grp)
    gmax = jnp.max(jnp.where(is_g, lg, neg), axis=1, keepdims=True)
    gsum = jnp.sum(jnp.where(is_g, jnp.exp(jnp.where(is_g, lg, neg) - gmax), 0.0), axis=1, keepdims=True)
    p_g = 1.0 / gsum
    g_idx = jnp.min(jnp.where(jnp.logical_and(is_g, lg == gmax), lane - n_exp, big), axis=1, keepdims=True)
    in_grp = jnp.logical_and(lane >= g_idx * epg, lane < (g_idx + 1.0) * epg)
    el = jnp.where(in_grp, lg, neg)
    m1 = jnp.max(el, axis=1, keepdims=True)
    i1 = jnp.min(jnp.where(jnp.logical_and(in_grp, lg == m1), lane, big), axis=1, keepdims=True)
    el2 = jnp.where(lane == i1, neg, el)
    m2 = jnp.max(el2, axis=1, keepdims=True)
    i2 = jnp.min(jnp.where(jnp.logical_and(el2 == m2, in_grp), lane, big), axis=1, keepdims=True)
    e2 = jnp.exp(m2 - m1)
    w1 = p_g / (1.0 + e2)
    w2 = p_g * e2 / (1.0 + e2)
    comb_ref[...] = (jnp.where(lane == 0.0, i1, 0.0) + jnp.where(lane == 1.0, i2, 0.0)
                     + jnp.where(lane == 2.0, w1, 0.0) + jnp.where(lane == 3.0, w2, 0.0))


def _mix(x_pt, o_a_ps, o_b_ps, proj, gate_off, wb, wo, ln_in, ln_mix, wr, br, *, rows, tm, alpha, n_exp, n_grp):
    d = x_pt[0].shape[1]
    w_a = o_a_ps[0].shape[1]
    tiles_p = x_pt[0].shape[0] // tm
    gw = math.gcd(gate_off, d)
    n_gc = d // gw
    gspec = lambda blk: pl.BlockSpec((tm, gw), lambda i, blk=blk: (i, blk))
    const = lambda a: pl.BlockSpec(a.shape, lambda i: (0,) * a.ndim, pipeline_mode=pl.Buffered(1))
    kern = functools.partial(_mix_kernel, n_gc=n_gc, gw=gw, w_a=w_a, alpha=alpha, n_exp=n_exp, n_grp=n_grp,
                             tiles_p=tiles_p)
    g0 = gate_off // gw
    args = (list(x_pt) + list(o_a_ps) + list(o_b_ps) + [proj] * (2 * n_gc)
            + [wb, wo, ln_in[0], ln_in[1], ln_mix[0], ln_mix[1], wr, br])
    in_specs = (_two_part_rows(tm, d, tiles_p) + _two_part_rows(tm, w_a, tiles_p)
                + _two_part_rows(tm, o_b_ps[0].shape[1], tiles_p)
                + [gspec(g0 + cidx) for cidx in range(n_gc)]
                + [gspec(g0 + n_gc + cidx) for cidx in range(n_gc)]
                + [const(a) for a in args[6 + 2 * n_gc:]])
    return pl.pallas_call(
        kern,
        grid=(rows // tm,),
        in_specs=in_specs,
        out_specs=[pl.BlockSpec((tm, d), lambda i: (i, 0)), pl.BlockSpec((tm, LANES), lambda i: (i, 0))],
        out_shape=[jax.ShapeDtypeStruct((rows, d), F32), jax.ShapeDtypeStruct((rows, LANES), F32)],
        scratch_shapes=[pltpu.VMEM((tm, d), BF16)],
        compiler_params=_params(("parallel",)),
        name="mix_out_router",
    )(*args)


def _gather_rows(src_hbm, idx_ref, base, n, dst, sem):
    def issue(r, carry):
        pltpu.make_async_copy(src_hbm.at[pl.ds(idx_ref[base + r], 1)], dst.at[pl.ds(r, 1)], sem).start()
        return carry

    lax.fori_loop(0, n, issue, 0, unroll=8)


def _wait_rows(src_hbm, n, dst, sem):
    pltpu.make_async_copy(src_hbm.at[pl.ds(0, n)], dst, sem).wait()


def _moe_ffn_kernel(te_ref, nu_ref, rt_ref, x_hbm, wg_ref, wu_ref, wd_ref, y_ref, xbuf, sem, wgb, wub, wdb, *, tm):
    t = pl.program_id(0)
    n_used = nu_ref[0]

    @pl.when(jnp.logical_and(t == 0, n_used > 0))
    def _():
        _gather_rows(x_hbm, rt_ref, 0, tm, xbuf.at[0], sem.at[0])

    @pl.when(t + 1 < n_used)
    def _():
        nxt = (t + 1) % 2
        _gather_rows(x_hbm, rt_ref, (t + 1) * tm, tm, xbuf.at[nxt], sem.at[nxt])

    @pl.when(t < n_used)
    def _():
        slot = t % 2
        _wait_rows(x_hbm, tm, xbuf.at[slot], sem.at[slot])

        @pl.when(jnp.logical_or(t == 0, te_ref[t] != te_ref[jnp.maximum(t - 1, 0)]))
        def _():
            wgb[...] = wg_ref[0].astype(BF16)
            wub[...] = wu_ref[0].astype(BF16)
            wdb[...] = wd_ref[0].astype(BF16)

        x = xbuf[slot].astype(BF16)
        hg = jnp.dot(x, wgb[...], preferred_element_type=F32)
        hu = jnp.dot(x, wub[...], preferred_element_type=F32)
        h = hg * _sigmoid(hg) * hu
        y_ref[...] = jnp.dot(h.astype(BF16), wdb[...], preferred_element_type=F32)

    @pl.when(t >= n_used)
    def _():
        y_ref[...] = jnp.zeros_like(y_ref)


def _moe_ffn(h1, tile_expert, n_used, row_token, wg, wu, wd, *, tm):
    d = h1.shape[1]
    n_exp, _, d_e = wg.shape
    n_tiles = tile_expert.shape[0]
    wspec = lambda shape: pl.BlockSpec((1,) + shape, lambda t, te, nu, rt: (te[t], 0, 0))
    return pl.pallas_call(
        functools.partial(_moe_ffn_kernel, tm=tm),
        grid_spec=pltpu.PrefetchScalarGridSpec(
            num_scalar_prefetch=3,
            grid=(n_tiles,),
            in_specs=[pl.BlockSpec(memory_space=pl.ANY), wspec((d, d_e)), wspec((d, d_e)), wspec((d_e, d))],
            out_specs=pl.BlockSpec((tm, d), lambda t, te, nu, rt: (t, 0)),
            scratch_shapes=[pltpu.VMEM((2, tm, d), F32), pltpu.SemaphoreType.DMA((2,)),
                            pltpu.VMEM((d, d_e), BF16), pltpu.VMEM((d, d_e), BF16), pltpu.VMEM((d_e, d), BF16)],
        ),
        out_shape=jax.ShapeDtypeStruct((n_tiles * tm, d), F32),
        compiler_params=pltpu.CompilerParams(dimension_semantics=("arbitrary",), vmem_limit_bytes=VMEM_LIMIT,
                                             disable_bounds_checks=True),
        name="moe_ffn",
    )(tile_expert, n_used, row_token, h1, wg, wu, wd)


def _moe_combine_kernel(pos_ref, y_hbm, x_ref, r_ref, g_ref, b_ref, op_ref, os_ref, buf, sem, *, tm, tiles_p, alpha):
    i = pl.program_id(0)
    n = pl.num_programs(0)

    def gather(tile, slot):
        def issue(r, carry):
            for k in range(TOP_K):
                p = pos_ref[TOP_K * (tile * tm + r) + k]
                pltpu.make_async_copy(y_hbm.at[pl.ds(p, 1)], buf.at[slot, pl.ds(k * tm + r, 1)], sem.at[slot]).start()
            return carry

        lax.fori_loop(0, tm, issue, 0, unroll=4)

    @pl.when(i == 0)
    def _():
        gather(0, 0)

    @pl.when(i + 1 < n)
    def _():
        gather(i + 1, (i + 1) % 2)

    slot = i % 2
    _wait_rows(y_hbm, TOP_K * tm, buf.at[slot], sem.at[slot])
    rec = r_ref[...]
    w1 = rec[:, 2:3]
    w2 = rec[:, 3:4]
    ffn = w1 * buf[slot, 0:tm, :] + w2 * buf[slot, tm:2 * tm, :]
    out = _layer_norm(alpha * x_ref[...] + ffn, g_ref[...], b_ref[...])

    @pl.when(i < tiles_p)
    def _():
        op_ref[...] = out

    @pl.when(i >= tiles_p)
    def _():
        os_ref[...] = out


def _moe_combine(pos, y_sorted, h1, route, ln_g, ln_b, *, n_p, tm, alpha):
    rows, d = h1.shape
    tiles_p = n_p // tm
    return pl.pallas_call(
        functools.partial(_moe_combine_kernel, tm=tm, tiles_p=tiles_p, alpha=alpha),
        grid_spec=pltpu.PrefetchScalarGridSpec(
            num_scalar_prefetch=1,
            grid=(rows // tm,),
            in_specs=[
                pl.BlockSpec(memory_space=pl.ANY),
                pl.BlockSpec((tm, d), lambda i, pos: (i, 0)),
                pl.BlockSpec((tm, LANES), lambda i, pos: (i, 0)),
                pl.BlockSpec((1, d), lambda i, pos: (0, 0)),
                pl.BlockSpec((1, d), lambda i, pos: (0, 0)),
            ],
            out_specs=[pl.BlockSpec((tm, d), lambda i, pos: (jnp.minimum(i, tiles_p - 1), 0)),
                       pl.BlockSpec((tm, d), lambda i, pos: (jnp.maximum(i - tiles_p, 0), 0))],
            scratch_shapes=[pltpu.VMEM((2, TOP_K * tm, d), F32), pltpu.SemaphoreType.DMA((2,))],
        ),
        out_shape=[jax.ShapeDtypeStruct((n_p, d), F32), jax.ShapeDtypeStruct((rows - n_p, d), F32)],
        compiler_params=pltpu.CompilerParams(dimension_semantics=("arbitrary",), vmem_limit_bytes=VMEM_LIMIT,
                                             disable_bounds_checks=True),
        name="moe_combine",
    )(pos, y_sorted, h1, route, ln_g, ln_b)


def _dispatch_plan(route, n_exp, tm):
    n_assign = route.shape[0] * TOP_K
    i32 = jnp.int32
    flat_e = route[:, :TOP_K].astype(i32).reshape(-1)
    order = jnp.argsort(flat_e, stable=True).astype(i32)
    rank = jnp.argsort(order).astype(i32)
    counts = jnp.sum((flat_e[:, None] == jnp.arange(n_exp, dtype=i32)[None, :]).astype(i32), axis=0)
    padded = (counts + tm - 1) // tm * tm
    ends_p = jnp.cumsum(padded)
    starts_p = ends_p - padded
    starts = jnp.cumsum(counts) - counts
    owner = lambda p: jnp.minimum(jnp.sum((p[:, None] >= ends_p[None, :]).astype(i32), axis=1), n_exp - 1)
    pos = rank - starts[flat_e] + starts_p[flat_e]
    n_tiles = (n_assign + n_exp * (tm - 1)) // tm + 1
    p = jnp.arange(n_tiles * tm, dtype=i32)
    e_p = owner(p)
    off = p - starts_p[e_p]
    src = jnp.clip(starts[e_p] + off, 0, n_assign - 1)
    row_token = jnp.where(off < counts[e_p], order[src] // TOP_K, 0)
    n_used = ends_p[-1] // tm
    tile_idx = jnp.minimum(jnp.arange(n_tiles, dtype=i32), n_used - 1)
    tile_expert = owner(tile_idx * tm)
    return tile_expert, n_used.reshape(1).astype(i32), row_token, pos


def _pad_cols(a, n):
    return jnp.pad(a, [(0, 0)] * (a.ndim - 1) + [(0, n - a.shape[-1])])


def kernel(x_prompt, x_sample, cache_sb_k, cache_sb_v, state_rwkv, state_rwkv_shift, meta_tokens, ln_in_g, ln_in_b, w_in, rwkv_mu, rwkv_w0, rwkv_w_up, rwkv_a0, rwkv_a_up, rwkv_g_up, rwkv_k_k, rwkv_k_a, rwkv_r_k, rwkv_lnx_g, rwkv_lnx_b, w_branch, w_out, ln_mix_g, ln_mix_b, router_group_w, router_group_b, router_expert_w, router_expert_b, moe_w_gate, moe_w_up, moe_w_down, ln_ffn_g, ln_ffn_b):
    depth = w_in.shape[0]
    assert depth == 1, "single-layer step only"
    nb, seq, d = x_prompt.shape
    db, ds = x_sample.shape[:2]
    n_meta = meta_tokens.shape[0]
    past = cache_sb_k.shape[2]
    w_a = rwkv_w0.shape[1]
    r_d, r_a, r_g = rwkv_w_up.shape[1], rwkv_a_up.shape[1], rwkv_g_up.shape[1]
    w_shift = 3 * w_a + r_d + r_a + r_g
    w_b = (w_in.shape[2] - w_shift - 2 * d) // 3
    h_a, h_b = w_a // HEAD_DIM, w_b // HEAD_DIM
    n_grp = router_group_w.shape[2]
    n_exp = router_expert_w.shape[2]
    alpha = (2 * depth) ** 0.25
    assert w_a % LANES == 0 and w_b % LANES == 0 and seq % 256 == 0 and ds == CHUNK and n_meta <= CHUNK
    assert n_exp + n_grp <= LANES

    dw, da, dg = (_round_up(n, LANES) for n in (r_d, r_a, r_g))
    pr_cols = 3 * w_a + dw + da + dg
    wi = w_in[0]
    s0 = 3 * w_a
    segs = [wi[:, :s0], _pad_cols(wi[:, s0:s0 + r_d], dw), _pad_cols(wi[:, s0 + r_d:s0 + r_d + r_a], da),
            _pad_cols(wi[:, s0 + r_d + r_a:w_shift], dg), wi[:, w_shift:]]
    w_pad = jnp.concatenate(segs, axis=1).astype(BF16)
    p_cols = w_pad.shape[1]
    q_off, k_off, v_off, gate_off = pr_cols, pr_cols + w_b, pr_cols + 2 * w_b, pr_cols + 3 * w_b

    def to_padded_rwkv_cols(a):
        return jnp.concatenate([a[..., :s0], _pad_cols(a[..., s0:s0 + r_d], dw),
                                _pad_cols(a[..., s0 + r_d:s0 + r_d + r_a], da),
                                _pad_cols(a[..., s0 + r_d + r_a:], dg)], axis=-1)

    def from_padded_rwkv_cols(a):
        return jnp.concatenate([a[..., :s0], a[..., s0:s0 + r_d], a[..., s0 + dw:s0 + dw + r_a],
                                a[..., s0 + dw + da:s0 + dw + da + r_g]], axis=-1)

    n_p, n_s = nb * seq, db * ds
    rows_main = n_p + n_s
    tm1 = _pick_tile(math.gcd(n_p, n_s), (512, 256, 128))
    rows_all = _round_up(rows_main + n_meta, tm1)
    x_p = x_prompt.reshape(n_p, d)
    x_t = jnp.concatenate([x_sample.reshape(n_s, d), meta_tokens.astype(F32),
                           jnp.zeros((rows_all - rows_main - n_meta, d), F32)], axis=0)
    tn1 = _pick_tile(p_cols, (1792, 512, 256, 128))
    proj = _inproj(x_p, x_t, ln_in_g[None], ln_in_b[None], w_pad, tm1, tn1)
    proj_meta = proj[rows_main:rows_main + n_meta]

    row2 = lambda a: a.reshape(1, -1)
    rw = dict(mu=to_padded_rwkv_cols(rwkv_mu[0])[None], w0=row2(rwkv_w0[0]), a0=row2(rwkv_a0[0]),
              k_k=row2(rwkv_k_k[0]), k_a=row2(rwkv_k_a[0]), r_k=row2(rwkv_r_k[0]),
              lnx_g=row2(rwkv_lnx_g[0]), lnx_b=row2(rwkv_lnx_b[0]),
              w_up=jnp.pad(rwkv_w_up[0], ((0, dw - r_d), (0, 0))),
              a_up=jnp.pad(rwkv_a_up[0], ((0, da - r_a), (0, 0))),
              g_up=jnp.pad(rwkv_g_up[0], ((0, dg - r_g), (0, 0))))
    n_pairs_a = w_a // LANES
    n_null = CHUNK - n_meta
    head = jnp.concatenate([jnp.zeros((n_null, pr_cols), F32), proj_meta[:, :pr_cols]], axis=0)
    tiles_p = seq // CHUNK
    o_a_p, hbd_p = _rwkv(proj, head, jnp.zeros((nb, 1, pr_cols), F32),
                         jnp.zeros((nb, n_pairs_a, LANES, LANES), F32), rw,
                         n_seq=nb, n_tiles=tiles_p + 1,
                         row_block=lambda s, t: s * tiles_p + jnp.maximum(t - 1, 0),
                         out_block=lambda s, t: s * tiles_p + jnp.maximum(t - 1, 0), out_rows=n_p, has_head=True, n_null=n_null, pr_cols=pr_cols, widths=(dw, da, dg))

    def to_hbd(state):
        n = state.shape[0]
        hkv = jnp.swapaxes(state, -1, -2).reshape(n, n_pairs_a, 2, HEAD_DIM, HEAD_DIM)
        z = jnp.zeros_like(hkv[:, :, 0])
        top = jnp.concatenate([hkv[:, :, 0], z], axis=-1)
        bot = jnp.concatenate([z, hkv[:, :, 1]], axis=-1)
        return jnp.concatenate([top, bot], axis=-2)

    def from_hbd(hbd):
        h0 = hbd[:, :, :HEAD_DIM, :HEAD_DIM]
        h1 = hbd[:, :, HEAD_DIM:, HEAD_DIM:]
        hkv = jnp.stack([h0, h1], axis=2).reshape(hbd.shape[0], h_a, HEAD_DIM, HEAD_DIM)
        return jnp.swapaxes(hkv, -1, -2)

    base_s = n_p // CHUNK
    o_a_s, hbd_s = _rwkv(proj, head, to_padded_rwkv_cols(state_rwkv_shift[0]), to_hbd(state_rwkv[0]), rw,
                         n_seq=db, n_tiles=ds // CHUNK,
                         row_block=lambda s, t: base_s + s * (ds // CHUNK) + t,
                         out_block=lambda s, t: s * (ds // CHUNK) + t, out_rows=n_s, has_head=False, n_null=0, pr_cols=pr_cols, widths=(dw, da, dg))

    bk = 256
    bq_p = _pick_tile(seq, (512, 256))
    assert past % bk == 0 and n_meta <= bk and ds <= bk
    n_pairs_b = w_b // LANES
    qc, kc, vc = q_off // LANES, k_off // LANES, v_off // LANES
    k_meta, v_meta = proj_meta[:, k_off:k_off + w_b], proj_meta[:, v_off:v_off + w_b]
    pad_rows = lambda a: jnp.pad(a, ((0, bk - a.shape[0]), (0, 0)))
    qt_p = seq // bq_p
    o_b_p, _ = _sb(proj, lambda b, p, i: (b * qt_p + i, qc + p),
                   proj, lambda b, p, i: (b, kc + p), proj, lambda b, p, i: (b, vc + p), seq,
                   pad_rows(k_meta), pad_rows(v_meta), lambda b, p, i: (0, p),
                   grid=(nb, n_pairs_b, qt_p), out_rows=n_p, out_block=lambda b, p, i: (b * qt_p + i, p),
                   bq=bq_p, bk=bk, extra_valid=n_meta, extra_newest=False)

    new = lambda off: proj[n_p:rows_main, off:off + w_b].reshape(db, ds, w_b)
    k_new, v_new = new(k_off), new(v_off)
    pad_seq = lambda a: jnp.pad(a, ((0, 0), (0, bk - ds), (0, 0))).reshape(db * bk, w_b)

    def sb_sample(first_key):
        n_keys = past - first_key
        rows = lambda c: c[0][:, first_key:].reshape(db * n_keys, w_b)
        return _sb(proj, lambda b, p, i: (n_p // ds + b, qc + p),
                   rows(cache_sb_k), lambda b, p, i: (b, p), rows(cache_sb_v), lambda b, p, i: (b, p), n_keys,
                   pad_seq(k_new), pad_seq(v_new), lambda b, p, i: (b, p),
                   grid=(db, n_pairs_b, 1), out_rows=n_s, out_block=lambda b, p, i: (b, p),
                   bq=ds, bk=bk, extra_valid=ds, extra_newest=True)

    recent = min(past, 2 * bk)
    o_b_s, c_min = sb_sample(past - recent)
    if recent < past:
        o_b_s = lax.cond(jnp.min(c_min) >= UNDERFLOW_LOG2, lambda: o_b_s, lambda: sb_sample(0)[0])

    wr = jnp.concatenate([router_expert_w[0], router_group_w[0]], axis=1)
    wr = _pad_cols(wr, LANES)
    br = _pad_cols(jnp.concatenate([router_expert_b[0], router_group_b[0]])[None], LANES)
    tm4 = _pick_tile(math.gcd(n_p, n_s), (256, 128))
    h1, route = _mix((x_p, x_t), (o_a_p, o_a_s), (o_b_p, o_b_s), proj, gate_off,
                     w_branch[0].astype(BF16), w_out[0].astype(BF16),
                     (ln_in_g[None], ln_in_b[None]), (ln_mix_g, ln_mix_b), wr, br,
                     rows=rows_main, tm=tm4, alpha=alpha, n_exp=n_exp, n_grp=n_grp)

    tm5 = _pick_tile(n_s, (256, 128))
    assert n_p % tm5 == 0
    tile_expert, n_used, row_token, pos = _dispatch_plan(route, n_exp, tm5)
    y_sorted = _moe_ffn(h1, tile_expert, n_used, row_token, moe_w_gate[0], moe_w_up[0], moe_w_down[0], tm=tm5)
    y_p, y_s = _moe_combine(pos, y_sorted, h1, route, ln_ffn_g, ln_ffn_b, n_p=n_p, tm=tm5, alpha=alpha)

    y_p = y_p.reshape(nb, seq, d)
    y_s = y_s.reshape(db, ds, d)
    heads = lambda a, n, t: a.reshape(1, n, t, h_b, HEAD_DIM)
    with_meta = lambda m, off: jnp.concatenate(
        [jnp.broadcast_to(m.reshape(1, 1, n_meta, h_b, HEAD_DIM), (1, nb, n_meta, h_b, HEAD_DIM)),
         proj[:n_p, off:off + w_b].reshape(1, nb, seq, h_b, HEAD_DIM)], axis=2)
    sb_k_p = with_meta(k_meta, k_off)
    sb_v_p = with_meta(v_meta, v_off)
    last_p = proj[seq - 1:n_p:seq, :pr_cols]
    last_s = proj[n_p + ds - 1:rows_main:ds, :pr_cols]
    shift_p = from_padded_rwkv_cols(last_p)[None, :, None, :]
    shift_s = from_padded_rwkv_cols(last_s)[None, :, None, :]
    return (y_p, y_s, sb_k_p, sb_v_p, from_hbd(hbd_p)[None], shift_p,
            heads(k_new, db, ds), heads(v_new, db, ds), from_hbd(hbd_s)[None], shift_s)
```
